```python
import math
import jax, jax.numpy as jnp
from jax import lax
import numpy as np

D_MODEL = 1024
BATCH = 2
SEQ = 8192
DEPTH = 1

D_SSM = D_MODEL
SSM_GROUP = 16
N_SSM_GROUPS = D_SSM // SSM_GROUP
SSM_STATE = 64
DT_MIN = 1e-3
DT_MAX = 1e-1
D_CONV = D_MODEL
CONV_WIDTH = 31
D_FF = ((8 * D_MODEL // 3 + 127) // 128) * 128
FFN_RESIDUAL_SCALE = 0.5
D_IN_PROJ = D_SSM + 2 * D_CONV + 2 * D_MODEL
RMS_EPS = 1e-6
LN_EPS = 1e-5

kernel_name = "macaron_s5_conformer_gated_hybrid"


def rms_norm(x, g):
    xf = x.astype(jnp.float32)
    y = xf * lax.rsqrt(jnp.mean(xf * xf, axis=-1, keepdims=True) + RMS_EPS)
    return (y * g.astype(jnp.float32)).astype(x.dtype)


def layer_norm(x, g, b):
    xf = x.astype(jnp.float32)
    mu = jnp.mean(xf, axis=-1, keepdims=True)
    xc = xf - mu
    var = jnp.mean(xc * xc, axis=-1, keepdims=True)
    y = xc * lax.rsqrt(var + LN_EPS) * g.astype(jnp.float32) + b.astype(jnp.float32)
    return y.astype(x.dtype)


def swiglu_ffn(h, w_up, w_down):
    gate, up = jnp.split(h @ w_up, 2, axis=-1)
    return (jax.nn.silu(gate) * up) @ w_down


def _ssm_combine(left, right):
    a1r, a1i, b1r, b1i = left
    a2r, a2i, b2r, b2i = right
    ar = a2r * a1r - a2i * a1i
    ai = a2r * a1i + a2i * a1r
    br = a2r * b1r - a2i * b1i + b2r
    bi = a2r * b1i + a2i * b1r + b2i
    return (ar, ai, br, bi)


def s5_ssm(u, a_re, a_im, log_dt, b_re, b_im, c_re, c_im, d_skip):
    f32 = jnp.float32
    bsz, seq, _ = u.shape
    ug = u.astype(f32).reshape(bsz, seq, N_SSM_GROUPS, SSM_GROUP)
    dt = jnp.exp(log_dt.astype(f32))[:, None]
    ar = a_re.astype(f32)
    ai = a_im.astype(f32)
    mag = jnp.exp(dt * ar)
    abar_re = mag * jnp.cos(dt * ai)
    abar_im = mag * jnp.sin(dt * ai)
    den = ar * ar + ai * ai
    nr = abar_re - 1.0
    ni = abar_im
    f_re = (nr * ar + ni * ai) / den
    f_im = (ni * ar - nr * ai) / den
    br_ = b_re.astype(f32)
    bi_ = b_im.astype(f32)
    bb_re = f_re[..., None] * br_ - f_im[..., None] * bi_
    bb_im = f_re[..., None] * bi_ + f_im[..., None] * br_
    bu_re = jnp.einsum('blgi,gpi->blgp', ug, bb_re)
    bu_im = jnp.einsum('blgi,gpi->blgp', ug, bb_im)
    a_seq_re = jnp.broadcast_to(abar_re[None, None], (1, seq, N_SSM_GROUPS, SSM_STATE))
    a_seq_im = jnp.broadcast_to(abar_im[None, None], (1, seq, N_SSM_GROUPS, SSM_STATE))
    _, _, s_re, s_im = lax.associative_scan(
        _ssm_combine, (a_seq_re, a_seq_im, bu_re, bu_im), axis=1)
    y = (jnp.einsum('blgp,gop->blgo', s_re, c_re.astype(f32))
         - jnp.einsum('blgp,gop->blgo', s_im, c_im.astype(f32)))
    y = y.reshape(bsz, seq, D_SSM) + d_skip.astype(f32) * u.astype(f32)
    return y.astype(u.dtype)


def conformer_conv(v, w_dw, b_dw, ln_g, ln_b, w_pw):
    val, gate = jnp.split(v, 2, axis=-1)
    z = val * jax.nn.sigmoid(gate)
    z = lax.conv_general_dilated(
        z, w_dw, window_strides=(1,), padding=[(CONV_WIDTH - 1, 0)],
        dimension_numbers=('NWC', 'WIO', 'NWC'),
        feature_group_count=D_CONV) + b_dw
    z = jax.nn.silu(layer_norm(z, ln_g, ln_b))
    return z @ w_pw


def setup_inputs(seed: int = 0) -> dict:
    key = jax.random.key(seed)
    ks = jax.random.split(key, 25)
    f32 = jnp.float32

    def nrm(k, shape, scale):
        return jax.random.normal(k, shape, f32) * scale

    def gain(k, shape):
        return 1.0 + 0.02 * jax.random.normal(k, shape, f32)

    n_idx = jnp.arange(SSM_STATE, dtype=f32)
    a_re = -0.5 + 0.01 * jax.random.normal(ks[6], (DEPTH, N_SSM_GROUPS, SSM_STATE), f32)
    a_im = math.pi * n_idx + 0.01 * jax.random.normal(ks[7], (DEPTH, N_SSM_GROUPS, SSM_STATE), f32)
    log_dt = jax.random.uniform(ks[8], (DEPTH, N_SSM_GROUPS), f32,
                                minval=math.log(DT_MIN), maxval=math.log(DT_MAX))
    cplx = math.sqrt(0.5)
    return {
        "x": jax.random.normal(ks[0], (BATCH, SEQ, D_MODEL), f32),
        "norm_ffn1_g": gain(ks[1], (DEPTH, D_MODEL)),
        "w_ffn1_up": nrm(ks[2], (DEPTH, D_MODEL, 2 * D_FF), D_MODEL ** -0.5),
        "w_ffn1_down": nrm(ks[3], (DEPTH, D_FF, D_MODEL), D_FF ** -0.5),
        "norm_mix_g": gain(ks[4], (DEPTH, D_MODEL)),
        "w_in": nrm(ks[5], (DEPTH, D_MODEL, D_IN_PROJ), D_MODEL ** -0.5),
        "ssm_a_re": a_re,
        "ssm_a_im": a_im,
        "ssm_log_dt": log_dt,
        "ssm_b_re": nrm(ks[9], (DEPTH, N_SSM_GROUPS, SSM_STATE, SSM_GROUP), cplx * SSM_GROUP ** -0.5),
        "ssm_b_im": nrm(ks[10], (DEPTH, N_SSM_GROUPS, SSM_STATE, SSM_GROUP), cplx * SSM_GROUP ** -0.5),
        "ssm_c_re": nrm(ks[11], (DEPTH, N_SSM_GROUPS, SSM_GROUP, SSM_STATE), cplx * SSM_STATE ** -0.5),
        "ssm_c_im": nrm(ks[12], (DEPTH, N_SSM_GROUPS, SSM_GROUP, SSM_STATE), cplx * SSM_STATE ** -0.5),
        "ssm_d": nrm(ks[13], (DEPTH, D_SSM), 1.0),
        "w_ssm_glu": nrm(ks[14], (DEPTH, D_SSM, 2 * D_MODEL), D_SSM ** -0.5),
        "conv_dw_w": nrm(ks[15], (DEPTH, CONV_WIDTH, 1, D_CONV), CONV_WIDTH ** -0.5),
        "conv_dw_b": nrm(ks[16], (DEPTH, D_CONV), 0.02),
        "conv_ln_g": gain(ks[17], (DEPTH, D_CONV)),
        "conv_ln_b": nrm(ks[18], (DEPTH, D_CONV), 0.02),
        "w_conv_pw": nrm(ks[19], (DEPTH, D_CONV, D_MODEL), D_CONV ** -0.5),
        "w_out": nrm(ks[20], (DEPTH, D_MODEL, D_MODEL), D_MODEL ** -0.5),
        "norm_ffn2_g": gain(ks[21], (DEPTH, D_MODEL)),
        "w_ffn2_up": nrm(ks[22], (DEPTH, D_MODEL, 2 * D_FF), D_MODEL ** -0.5),
        "w_ffn2_down": nrm(ks[23], (DEPTH, D_FF, D_MODEL), D_FF ** -0.5),
        "norm_final_g": gain(ks[24], (D_MODEL,)),
    }


def reference(x, norm_ffn1_g, w_ffn1_up, w_ffn1_down, norm_mix_g, w_in,
              ssm_a_re, ssm_a_im, ssm_log_dt, ssm_b_re, ssm_b_im, ssm_c_re, ssm_c_im,
              ssm_d, w_ssm_glu, conv_dw_w, conv_dw_b, conv_ln_g, conv_ln_b, w_conv_pw,
              w_out, norm_ffn2_g, w_ffn2_up, w_ffn2_down, norm_final_g):
    for l in range(DEPTH):
        h = rms_norm(x, norm_ffn1_g[l])
        x = x + FFN_RESIDUAL_SCALE * swiglu_ffn(h, w_ffn1_up[l], w_ffn1_down[l])

        h = rms_norm(x, norm_mix_g[l])
        proj = h @ w_in[l]
        u_ssm, v_conv, gate_logits = jnp.split(proj, [D_SSM, D_SSM + 2 * D_CONV], axis=-1)

        y_ssm = s5_ssm(u_ssm, ssm_a_re[l], ssm_a_im[l], ssm_log_dt[l], ssm_b_re[l],
                       ssm_b_im[l], ssm_c_re[l], ssm_c_im[l], ssm_d[l])
        glu_val, glu_gate = jnp.split(jax.nn.gelu(y_ssm) @ w_ssm_glu[l], 2, axis=-1)
        y_a = glu_val * jax.nn.sigmoid(glu_gate)

        y_b = conformer_conv(v_conv, conv_dw_w[l], conv_dw_b[l], conv_ln_g[l],
                             conv_ln_b[l], w_conv_pw[l])

        g_a, g_b = jnp.split(jax.nn.sigmoid(gate_logits), 2, axis=-1)
        x = x + (g_a * y_a + g_b * y_b) @ w_out[l]

        h = rms_norm(x, norm_ffn2_g[l])
        x = x + FFN_RESIDUAL_SCALE * swiglu_ffn(h, w_ffn2_up[l], w_ffn2_down[l])
    return rms_norm(x, norm_final_g)
```

```python
import functools
import math

import jax
import jax.numpy as jnp
from jax import lax
from jax.experimental import pallas as pl
from jax.experimental.pallas import tpu as pltpu

D_MODEL = 1024
D_FF = 2816
SSM_GROUP = 16
N_GROUPS = 64
SSM_STATE = 64
CONV_WIDTH = 31
RMS_EPS = 1e-6
LN_EPS = 1e-5
FFN_SCALE = 0.5

LANES = 128
SUBLANES = 8
MXU_DIM = 256
VMEM_LIMIT_BYTES = 56 * 1024 * 1024

PAIRS_PER_SG = SUBLANES
N_SG = N_GROUPS // (2 * PAIRS_PER_SG)
SG_IN = 2 * SSM_GROUP * PAIRS_PER_SG
SG_STATE = 2 * LANES * PAIRS_PER_SG
CONV_HALO = 32

FFN_CHUNKS = ((0, 1536), (1536, D_FF))

BF16 = jnp.bfloat16
F32 = jnp.float32


def _dot(a, b):
    return jnp.dot(a, b, preferred_element_type=F32)


def _rms_norm(x, g):
    return x * lax.rsqrt(jnp.mean(x * x, axis=-1, keepdims=True) + RMS_EPS) * g


def _sigmoid(x):
    return 1.0 / (1.0 + jnp.exp(-x))


def _swiglu(h, wup_ref, wdn_ref):
    acc = None
    for c0, c1 in FFN_CHUNKS:
        gate = _dot(h, wup_ref[:, c0:c1])
        up = _dot(h, wup_ref[:, D_FF + c0:D_FF + c1])
        act = (gate * _sigmoid(gate) * up).astype(BF16)
        part = _dot(act, wdn_ref[c0:c1, :])
        acc = part if acc is None else acc + part
    return acc


def _ffn_inproj_kernel(x_ref, g1_ref, wup_ref, wdn_ref, gm_ref, win_ref,
                       x1_ref, u_ref, v_ref, gl_ref):
    x = x_ref[...]
    h = _rms_norm(x, g1_ref[...]).astype(BF16)
    x1 = x + FFN_SCALE * _swiglu(h, wup_ref, wdn_ref)
    x1_ref[...] = x1
    h2 = _rms_norm(x1, gm_ref[...]).astype(BF16)
    u_ref[...] = _dot(h2, win_ref[:, 0:D_MODEL]).astype(u_ref.dtype)
    v_ref[...] = _dot(h2, win_ref[:, D_MODEL:3 * D_MODEL]).astype(v_ref.dtype)
    gl_ref[...] = _dot(h2, win_ref[:, 3 * D_MODEL:5 * D_MODEL]).astype(gl_ref.dtype)


def _const_spec(shape):
    nd = len(shape)
    return pl.BlockSpec(shape, lambda *_: (0,) * nd, pipeline_mode=pl.Buffered(1))


def _ffn_inproj(x2d, g1, wup, wdn, gm, win, tm):
    n_tok = x2d.shape[0]
    row = lambda w: pl.BlockSpec((tm, w), lambda i: (i, 0))
    return pl.pallas_call(
        _ffn_inproj_kernel,
        grid=(n_tok // tm,),
        in_specs=[row(D_MODEL), _const_spec(g1.shape), _const_spec(wup.shape),
                  _const_spec(wdn.shape), _const_spec(gm.shape), _const_spec(win.shape)],
        out_specs=[row(D_MODEL), row(D_MODEL), row(2 * D_MODEL), row(2 * D_MODEL)],
        out_shape=[jax.ShapeDtypeStruct((n_tok, D_MODEL), F32),
                   jax.ShapeDtypeStruct((n_tok, D_MODEL), BF16),
                   jax.ShapeDtypeStruct((n_tok, 2 * D_MODEL), BF16),
                   jax.ShapeDtypeStruct((n_tok, 2 * D_MODEL), BF16)],
        compiler_params=pltpu.CompilerParams(
            dimension_semantics=("arbitrary",), vmem_limit_bytes=VMEM_LIMIT_BYTES),
        name="ffn_inproj",
    )(x2d, g1, wup, wdn, gm, win)


def _ssm_glu_kernel(u_ref, bw_ref, ar_ref, ai_ref, cw_ref, d_ref, wglu_ref,
                    ya_ref, sre_ref, sim_ref, carry_ref, *, tb):
    @pl.when(pl.program_id(1) == 0)
    def _():
        carry_ref[...] = jnp.zeros_like(carry_ref)

    u = u_ref[0]

    for sg in range(N_SG):
        bu = _dot(u[:, sg * SG_IN:(sg + 1) * SG_IN], bw_ref[sg])
        for j in range(PAIRS_PER_SG):
            c0 = 2 * LANES * j
            sre_ref[sg, pl.ds(j, tb, stride=PAIRS_PER_SG), :] = bu[:, c0:c0 + LANES]
            sim_ref[sg, pl.ds(j, tb, stride=PAIRS_PER_SG), :] = bu[:, c0 + LANES:c0 + 2 * LANES]

    ar = [ar_ref[sg] for sg in range(N_SG)]
    ai = [ai_ref[sg] for sg in range(N_SG)]

    def step(t, carry):
        r0 = pl.multiple_of(t * SUBLANES, SUBLANES)
        new = []
        for sg in range(N_SG):
            sr, si = carry[2 * sg], carry[2 * sg + 1]
            nr = ar[sg] * sr - ai[sg] * si + sre_ref[sg, pl.ds(r0, SUBLANES), :]
            ni = ar[sg] * si + ai[sg] * sr + sim_ref[sg, pl.ds(r0, SUBLANES), :]
            sre_ref[sg, pl.ds(r0, SUBLANES), :] = nr
            sim_ref[sg, pl.ds(r0, SUBLANES), :] = ni
            new += [nr, ni]
        return tuple(new)

    init = tuple(carry_ref[i] for i in range(2 * N_SG))
    final = lax.fori_loop(0, tb, step, init, unroll=4)
    for i in range(2 * N_SG):
        carry_ref[i] = final[i]

    ys = []
    for sg in range(N_SG):
        acc = None
        for j in range(PAIRS_PER_SG):
            s_j = jnp.concatenate(
                [sre_ref[sg, pl.ds(j, tb, stride=PAIRS_PER_SG), :],
                 sim_ref[sg, pl.ds(j, tb, stride=PAIRS_PER_SG), :]], axis=1).astype(BF16)
            part = _dot(s_j, cw_ref[sg, j])
            acc = part if acc is None else acc + part
        ys.append(acc)
    y = jnp.concatenate(ys, axis=1) + d_ref[...] * u.astype(F32)

    act = jax.nn.gelu(y).astype(BF16)
    glu = _dot(act, wglu_ref[...])
    ya = glu[:, :D_MODEL] * _sigmoid(glu[:, D_MODEL:])
    ya_ref[0] = ya.astype(ya_ref.dtype)


def _ssm_glu(u3d, bw, ar, ai, cw, d, wglu, tb):
    bsz, seq, _ = u3d.shape
    blk = lambda w: pl.BlockSpec((1, tb, w), lambda b, i: (b, i, 0))
    return pl.pallas_call(
        functools.partial(_ssm_glu_kernel, tb=tb),
        grid=(bsz, seq // tb),
        in_specs=[blk(D_MODEL), _const_spec(bw.shape), _const_spec(ar.shape),
                  _const_spec(ai.shape), _const_spec(cw.shape), _const_spec(d.shape),
                  _const_spec(wglu.shape)],
        out_specs=blk(D_MODEL),
        out_shape=jax.ShapeDtypeStruct((bsz, seq, D_MODEL), BF16),
        scratch_shapes=[pltpu.VMEM((N_SG, tb * PAIRS_PER_SG, LANES), F32),
                        pltpu.VMEM((N_SG, tb * PAIRS_PER_SG, LANES), F32),
                        pltpu.VMEM((2 * N_SG, SUBLANES, LANES), F32)],
        compiler_params=pltpu.CompilerParams(
            dimension_semantics=("arbitrary", "arbitrary"),
            vmem_limit_bytes=VMEM_LIMIT_BYTES),
        name="ssm_glu",
    )(u3d, bw, ar, ai, cw, d, wglu)


def _mix_ffn_kernel(x1_ref, v_ref, gl_ref, ya_ref, cw_ref, cb_ref, lng_ref, lnb_ref,
                    wpw_ref, wout_ref, g2_ref, wup_ref, wdn_ref, gf_ref,
                    o_ref, z_ref, *, tb, final_norm):
    @pl.when(pl.program_id(1) == 0)
    def _():
        z_ref[0:CONV_HALO, :] = jnp.zeros((CONV_HALO, D_MODEL), F32)

    v = v_ref[0].astype(F32)
    z_ref[CONV_HALO:CONV_HALO + tb, :] = v[:, :D_MODEL] * _sigmoid(v[:, D_MODEL:])

    base = CONV_HALO - (CONV_WIDTH - 1)
    acc = jnp.broadcast_to(cb_ref[...], (tb, D_MODEL))
    for k in range(CONV_WIDTH):
        acc = acc + cw_ref[k:k + 1, :] * z_ref[base + k:base + k + tb, :]
    z_ref[0:CONV_HALO, :] = z_ref[tb:tb + CONV_HALO, :]

    mu = jnp.mean(acc, axis=-1, keepdims=True)
    xc = acc - mu
    var = jnp.mean(xc * xc, axis=-1, keepdims=True)
    ln = xc * lax.rsqrt(var + LN_EPS) * lng_ref[...] + lnb_ref[...]
    yb = _dot((ln * _sigmoid(ln)).astype(BF16), wpw_ref[...])

    gates = _sigmoid(gl_ref[0].astype(F32))
    mix = gates[:, :D_MODEL] * ya_ref[0].astype(F32) + gates[:, D_MODEL:] * yb
    x2 = x1_ref[0] + _dot(mix.astype(BF16), wout_ref[...])

    h = _rms_norm(x2, g2_ref[...]).astype(BF16)
    x3 = x2 + FFN_SCALE * _swiglu(h, wup_ref, wdn_ref)
    o_ref[0] = _rms_norm(x3, gf_ref[...]) if final_norm else x3


def _mix_ffn(x1, v, gl, ya, cw, cb, lng, lnb, wpw, wout, g2, wup, wdn, gf, tb, final_norm):
    bsz, seq, _ = x1.shape
    blk = lambda w: pl.BlockSpec((1, tb, w), lambda b, i: (b, i, 0))
    consts = (cw, cb, lng, lnb, wpw, wout, g2, wup, wdn, gf)
    return pl.pallas_call(
        functools.partial(_mix_ffn_kernel, tb=tb, final_norm=final_norm),
        grid=(bsz, seq // tb),
        in_specs=[blk(D_MODEL), blk(2 * D_MODEL), blk(2 * D_MODEL), blk(D_MODEL)]
                 + [_const_spec(c.shape) for c in consts],
        out_specs=blk(D_MODEL),
        out_shape=jax.ShapeDtypeStruct((bsz, seq, D_MODEL), F32),
        scratch_shapes=[pltpu.VMEM((CONV_HALO + tb, D_MODEL), F32)],
        compiler_params=pltpu.CompilerParams(
            dimension_semantics=("arbitrary", "arbitrary"),
            vmem_limit_bytes=VMEM_LIMIT_BYTES),
        name="mix_ffn",
    )(x1, v, gl, ya, *consts)


def _ssm_params(a_re, a_im, log_dt, b_re, b_im, c_re, c_im):
    dt = jnp.exp(log_dt)[:, None]
    mag = jnp.exp(dt * a_re)
    abar_re = mag * jnp.cos(dt * a_im)
    abar_im = mag * jnp.sin(dt * a_im)
    den = a_re * a_re + a_im * a_im
    nr = abar_re - 1.0
    ni = abar_im
    f_re = (nr * a_re + ni * a_im) / den
    f_im = (ni * a_re - nr * a_im) / den
    bb_re = f_re[..., None] * b_re - f_im[..., None] * b_im
    bb_im = f_re[..., None] * b_im + f_im[..., None] * b_re

    eye_j = jnp.eye(PAIRS_PER_SG, dtype=F32)
    eye_e = jnp.eye(2, dtype=F32)

    def b_block(bb):
        bb = bb.reshape(N_SG, PAIRS_PER_SG, 2, SSM_STATE, SSM_GROUP)
        return jnp.einsum('sjepi,jk,ef->sjeikfp', bb, eye_j, eye_e)

    bw = jnp.stack([b_block(bb_re), b_block(bb_im)], axis=5)
    bw = bw.reshape(N_SG, SG_IN, SG_STATE).astype(BF16)

    def c_block(c):
        c = c.reshape(N_SG, PAIRS_PER_SG, 2, SSM_GROUP, SSM_STATE)
        return jnp.einsum('sjeop,jk,ef->sjepkfo', c, eye_j, eye_e)

    cw = jnp.stack([c_block(c_re), -c_block(c_im)], axis=2)
    cw = cw.reshape(N_SG, PAIRS_PER_SG, 2 * LANES, MXU_DIM).astype(BF16)

    ar = abar_re.reshape(N_SG, PAIRS_PER_SG, LANES)
    ai = abar_im.reshape(N_SG, PAIRS_PER_SG, LANES)
    return bw, ar, ai, cw


def kernel(x, norm_ffn1_g, w_ffn1_up, w_ffn1_down, norm_mix_g, w_in, ssm_a_re, ssm_a_im, ssm_log_dt, ssm_b_re, ssm_b_im, ssm_c_re, ssm_c_im, ssm_d, w_ssm_glu, conv_dw_w, conv_dw_b, conv_ln_g, conv_ln_b, w_conv_pw, w_out, norm_ffn2_g, w_ffn2_up, w_ffn2_down, norm_final_g):
    bsz, seq, d = x.shape
    depth = w_in.shape[0]
    assert d == D_MODEL and w_ffn1_up.shape[-1] == 2 * D_FF
    tm = 256
    tb = 256
    assert (bsz * seq) % tm == 0 and seq % tb == 0
    row = lambda a: a.reshape(1, -1).astype(F32)

    for l in range(depth):
        x1, u, v, gl = _ffn_inproj(
            x.reshape(bsz * seq, d), row(norm_ffn1_g[l]), w_ffn1_up[l].astype(BF16),
            w_ffn1_down[l].astype(BF16), row(norm_mix_g[l]), w_in[l].astype(BF16), tm)
        bw, ar, ai, cw = _ssm_params(ssm_a_re[l], ssm_a_im[l], ssm_log_dt[l], ssm_b_re[l],
                                     ssm_b_im[l], ssm_c_re[l], ssm_c_im[l])
        ya = _ssm_glu(u.reshape(bsz, seq, d), bw, ar, ai, cw, row(ssm_d[l]),
                      w_ssm_glu[l].astype(BF16), tb)
        conv_w = jnp.pad(conv_dw_w[l].reshape(CONV_WIDTH, d), ((0, 1), (0, 0)))
        x = _mix_ffn(x1.reshape(bsz, seq, d), v.reshape(bsz, seq, 2 * d),
                     gl.reshape(bsz, seq, 2 * d), ya, conv_w, row(conv_dw_b[l]),
                     row(conv_ln_g[l]), row(conv_ln_b[l]), w_conv_pw[l].astype(BF16),
                     w_out[l].astype(BF16), row(norm_ffn2_g[l]), w_ffn2_up[l].astype(BF16),
                     w_ffn2_down[l].astype(BF16), row(norm_final_g), tb,
                     final_norm=(l == depth - 1))
    return x
```

```python
import functools

import jax
import jax.numpy as jnp
from jax import lax
from jax.experimental import pallas as pl
from jax.experimental.pallas import tpu as pltpu

D_MODEL = 1024
D_FF = 2816
SSM_GROUP = 16
N_GROUPS = 64
SSM_STATE = 64
CONV_WIDTH = 31
RMS_EPS = 1e-6
LN_EPS = 1e-5
FFN_SCALE = 0.5

LANES = 128
SUBLANES = 8
MXU_DIM = 256
VMEM_LIMIT_BYTES = 56 * 1024 * 1024

PAIRS_PER_SG = SUBLANES
N_SG = N_GROUPS // (2 * PAIRS_PER_SG)
SG_IN = 2 * SSM_GROUP * PAIRS_PER_SG
SG_STATE = 2 * LANES * PAIRS_PER_SG
TOKEN_PITCH = 12
N_SLABS = D_MODEL // LANES
CONV_HALO = 32

FFN_CHUNKS = ((0, 1536), (1536, D_FF))

BF16 = jnp.bfloat16
F32 = jnp.float32


def _dot(a, b):
    return jnp.dot(a, b, preferred_element_type=F32)


def _rms_norm(x, g):
    return x * lax.rsqrt(jnp.mean(x * x, axis=-1, keepdims=True) + RMS_EPS) * g


def _sigmoid(x):
    return 1.0 / (1.0 + jnp.exp(-x))


def _zero_of(x):
    bits = lax.bitcast_convert_type(x, jnp.uint32)
    return lax.bitcast_convert_type((bits >> 16) >> 16, F32)


def _tile_sum(a):
    rows, cols = a.shape
    acc = None
    for r in range(0, rows, SUBLANES):
        for c in range(0, cols, LANES):
            t = a[r:r + SUBLANES, c:c + LANES]
            acc = t if acc is None else acc + t
    return acc


def _with_dep(a_bf16, a_f32, dep):
    rows = a_f32.shape[0]
    r0 = rows - 2 * SUBLANES
    z = _zero_of(dep)
    corner = (a_f32[r0:, 0:LANES] + jnp.concatenate([z, z], axis=0)).astype(BF16)
    bottom = jnp.concatenate([corner, a_bf16[r0:, LANES:]], axis=1)
    return jnp.concatenate([a_bf16[:r0], bottom], axis=0)


def _swiglu(h, wup_ref, wdn_ref):
    acc = None
    for c0, c1 in FFN_CHUNKS:
        gate = _dot(h, wup_ref[:, c0:c1])
        up = _dot(h, wup_ref[:, D_FF + c0:D_FF + c1])
        act = (gate * _sigmoid(gate) * up).astype(BF16)
        part = _dot(act, wdn_ref[c0:c1, :])
        acc = part if acc is None else acc + part
    return acc


def _const_spec(shape):
    nd = len(shape)
    return pl.BlockSpec(shape, lambda *_: (0,) * nd, pipeline_mode=pl.Buffered(1))


def _ffn_conv_kernel(x_ref, g1_ref, wup_ref, wdn_ref, gm_ref, win_ref,
                     cw_ref, cb_ref, lng_ref, lnb_ref,
                     x1_ref, u_ref, gl_ref, c_ref, z_ref, cv_ref, *, tm, blocks_per_seq):
    g = pl.program_id(0)

    @pl.when(g == 0)
    def _():
        z_ref[...] = jnp.zeros_like(z_ref)

    base = CONV_HALO - (CONV_WIDTH - 1)

    def conv_slab(c):
        lanes = slice(c * LANES, (c + 1) * LANES)
        acc = jnp.broadcast_to(cb_ref[:, lanes], (tm, LANES))
        for k in range(CONV_WIDTH):
            acc = acc + cw_ref[k:k + 1, lanes] * z_ref[c, pl.ds(base + k, tm, stride=1), :]
        cv_ref[:, lanes] = acc
        return _tile_sum(acc)

    x = x_ref[...]
    hf = _rms_norm(x, g1_ref[...])
    h = hf.astype(BF16)
    ffn = None
    dep = None
    for i, (c0, c1) in enumerate(FFN_CHUNKS):
        gate = _dot(h if dep is None else _with_dep(h, hf, dep), wup_ref[:, c0:c1])
        dep = conv_slab(3 * i)
        up = _dot(_with_dep(h, hf, dep), wup_ref[:, D_FF + c0:D_FF + c1])
        dep = conv_slab(3 * i + 1)
        actf = gate * _sigmoid(gate) * up
        part = _dot(_with_dep(actf.astype(BF16), actf, dep), wdn_ref[c0:c1, :])
        dep = conv_slab(3 * i + 2)
        ffn = part if ffn is None else ffn + part
    x1 = x + FFN_SCALE * ffn
    x1_ref[...] = x1
    h2f = _rms_norm(x1, gm_ref[...])
    h2 = h2f.astype(BF16)
    u_ref[...] = _dot(_with_dep(h2, h2f, dep), win_ref[:, 0:D_MODEL]).astype(u_ref.dtype)
    dep = conv_slab(6)
    gl_ref[...] = _dot(_with_dep(h2, h2f, dep),
                       win_ref[:, 3 * D_MODEL:5 * D_MODEL]).astype(gl_ref.dtype)
    conv_slab(7)

    cv = cv_ref[...]
    mu = jnp.mean(cv, axis=-1, keepdims=True)
    xc = cv - mu
    var = jnp.mean(xc * xc, axis=-1, keepdims=True)
    ln = xc * lax.rsqrt(var + LN_EPS) * lng_ref[...] + lnb_ref[...]
    cact = ln * _sigmoid(ln)
    c_ref[...] = cact.astype(c_ref.dtype)

    v = _dot(_with_dep(h2, h2f, _tile_sum(cact)), win_ref[:, D_MODEL:3 * D_MODEL])
    z = v[:, :D_MODEL] * _sigmoid(v[:, D_MODEL:])

    seq_start = (g % blocks_per_seq) == 0
    for c in range(N_SLABS):
        tail = z_ref[c, tm:tm + CONV_HALO, :]
        z_ref[c, 0:CONV_HALO, :] = jnp.where(seq_start, 0.0, tail)
        z_ref[c, CONV_HALO:CONV_HALO + tm, :] = z[:, c * LANES:(c + 1) * LANES]


def _ffn_conv(x2d, g1, wup, wdn, gm, win, cw, cb, lng, lnb, tm, blocks_per_seq):
    n_tok = x2d.shape[0]
    n_blk = n_tok // tm
    cur = lambda w: pl.BlockSpec((tm, w), lambda i: (jnp.minimum(i, n_blk - 1), 0))
    prev = lambda w: pl.BlockSpec((tm, w), lambda i: (jnp.maximum(i - 1, 0), 0))
    consts = (g1, wup, wdn, gm, win, cw, cb, lng, lnb)
    return pl.pallas_call(
        functools.partial(_ffn_conv_kernel, tm=tm, blocks_per_seq=blocks_per_seq),
        grid=(n_blk + 1,),
        in_specs=[cur(D_MODEL)] + [_const_spec(c.shape) for c in consts],
        out_specs=[cur(D_MODEL), cur(D_MODEL), cur(2 * D_MODEL), prev(D_MODEL)],
        out_shape=[jax.ShapeDtypeStruct((n_tok, D_MODEL), F32),
                   jax.ShapeDtypeStruct((n_tok, D_MODEL), BF16),
                   jax.ShapeDtypeStruct((n_tok, 2 * D_MODEL), BF16),
                   jax.ShapeDtypeStruct((n_tok, D_MODEL), BF16)],
        scratch_shapes=[pltpu.VMEM((N_SLABS, CONV_HALO + tm, LANES), F32),
                        pltpu.VMEM((tm, D_MODEL), F32)],
        compiler_params=pltpu.CompilerParams(
            dimension_semantics=("arbitrary",), vmem_limit_bytes=VMEM_LIMIT_BYTES),
        name="ffn_conv",
    )(x2d, *consts)


def _bu_scatter(u_sg, bw_sg, re_ref, im_ref, sg, th):
    bu = _dot(u_sg, bw_sg)
    for j in range(PAIRS_PER_SG):
        c0 = 2 * LANES * j
        re_ref[sg, pl.ds(j, th, stride=TOKEN_PITCH), :] = bu[:, c0:c0 + LANES]
        im_ref[sg, pl.ds(j, th, stride=TOKEN_PITCH), :] = bu[:, c0 + LANES:c0 + 2 * LANES]


def _ssm_glu_kernel(u_ref, un_ref, bw_ref, ar_ref, ai_ref, cw_ref, d_ref, wglu_ref,
                    ya_ref, are_ref, aim_ref, bre_ref, bim_ref, carry_ref, *, th):
    scan_tokens = th // N_SG

    @pl.when(pl.program_id(1) == 0)
    def _():
        carry_ref[...] = jnp.zeros_like(carry_ref)
        for sg in range(N_SG):
            _bu_scatter(u_ref[0, 0:th, sg * SG_IN:(sg + 1) * SG_IN], bw_ref[sg],
                        are_ref, aim_ref, sg, th)

    ar = [ar_ref[sg] for sg in range(N_SG)]
    ai = [ai_ref[sg] for sg in range(N_SG)]

    def recur_and_project(cur_re, cur_im, nxt_re, nxt_im, load_next_u):
        def quarter(q, carry):
            row0 = q * (scan_tokens * TOKEN_PITCH)
            carry = list(carry)
            u_q = load_next_u(q)
            steps_per_pair = scan_tokens // PAIRS_PER_SG
            deps = []
            for j in range(PAIRS_PER_SG):
                if j >= 2:
                    carry[0] = carry[0] + _zero_of(deps[j - 2])
                for t in range(j * steps_per_pair, (j + 1) * steps_per_pair):
                    rows = pl.ds(row0 + t * TOKEN_PITCH, PAIRS_PER_SG, stride=1)
                    for sg in range(N_SG):
                        sr, si = carry[2 * sg], carry[2 * sg + 1]
                        nr = ar[sg] * sr - ai[sg] * si + cur_re[sg, rows, :]
                        ni = ar[sg] * si + ai[sg] * sr + cur_im[sg, rows, :]
                        cur_re[sg, rows, :] = nr
                        cur_im[sg, rows, :] = ni
                        carry[2 * sg], carry[2 * sg + 1] = nr, ni
                bu = _dot(u_q, bw_ref[q, :, 2 * LANES * j:2 * LANES * (j + 1)])
                deps.append(bu[0:SUBLANES, 0:LANES])
                nxt_re[q, pl.ds(j, th, stride=TOKEN_PITCH), :] = bu[:, :LANES]
                nxt_im[q, pl.ds(j, th, stride=TOKEN_PITCH), :] = bu[:, LANES:]
            return tuple(carry)

        init = tuple(carry_ref[i] for i in range(2 * N_SG))
        final = init
        for q in range(N_SG):
            final = quarter(q, final)
        for i in range(2 * N_SG):
            carry_ref[i] = final[i]

    def output(s_re, s_im, u_half):
        ys = []
        for sg in range(N_SG):
            parts = []
            for j in range(PAIRS_PER_SG):
                parts += [s_re[sg, pl.ds(j, th, stride=TOKEN_PITCH), :].astype(BF16),
                          s_im[sg, pl.ds(j, th, stride=TOKEN_PITCH), :].astype(BF16)]
            ys.append(_dot(jnp.concatenate(parts, axis=1), cw_ref[sg]))
        y = jnp.concatenate(ys, axis=1) + d_ref[...] * u_half.astype(F32)
        glu = _dot(jax.nn.gelu(y).astype(BF16), wglu_ref[...])
        return (glu[:, :D_MODEL] * _sigmoid(glu[:, D_MODEL:])).astype(ya_ref.dtype)

    def u_second(q):
        return u_ref[0, th:2 * th, q * SG_IN:(q + 1) * SG_IN]

    def u_next(q):
        return un_ref[0, :, q * SG_IN:(q + 1) * SG_IN]

    recur_and_project(are_ref, aim_ref, bre_ref, bim_ref, u_second)
    ya_ref[0, 0:th, :] = output(are_ref, aim_ref, u_ref[0, 0:th, :])
    recur_and_project(bre_ref, bim_ref, are_ref, aim_ref, u_next)
    ya_ref[0, th:2 * th, :] = output(bre_ref, bim_ref, u_ref[0, th:2 * th, :])


def _ssm_glu(u3d, bw, ar, ai, cw, d, wglu, th):
    bsz, seq, _ = u3d.shape
    n_half = seq // th
    consts = (bw, ar, ai, cw, d, wglu)
    state = pltpu.VMEM((N_SG, th * TOKEN_PITCH, LANES), F32)
    return pl.pallas_call(
        functools.partial(_ssm_glu_kernel, th=th),
        grid=(bsz, n_half // 2),
        in_specs=[pl.BlockSpec((1, 2 * th, D_MODEL), lambda b, i: (b, i, 0)),
                  pl.BlockSpec((1, th, D_MODEL),
                               lambda b, i: (b, jnp.minimum(2 * i + 2, n_half - 1), 0))]
                 + [_const_spec(c.shape) for c in consts],
        out_specs=pl.BlockSpec((1, 2 * th, D_MODEL), lambda b, i: (b, i, 0)),
        out_shape=jax.ShapeDtypeStruct((bsz, seq, D_MODEL), BF16),
        scratch_shapes=[state, state, state, state,
                        pltpu.VMEM((2 * N_SG, SUBLANES, LANES), F32)],
        compiler_params=pltpu.CompilerParams(
            dimension_semantics=("arbitrary", "arbitrary"),
            vmem_limit_bytes=VMEM_LIMIT_BYTES),
        name="ssm_glu",
    )(u3d, u3d, *consts)


def _mix_ffn_kernel(x1_ref, c_ref, gl_ref, ya_ref, wpw_ref, wout_ref, g2_ref, wup_ref,
                    wdn_ref, gf_ref, o_ref, *, final_norm):
    yb = _dot(c_ref[...], wpw_ref[...])
    gates = _sigmoid(gl_ref[...].astype(F32))
    mix = gates[:, :D_MODEL] * ya_ref[...].astype(F32) + gates[:, D_MODEL:] * yb
    x2 = x1_ref[...] + _dot(mix.astype(BF16), wout_ref[...])
    h = _rms_norm(x2, g2_ref[...]).astype(BF16)
    x3 = x2 + FFN_SCALE * _swiglu(h, wup_ref, wdn_ref)
    o_ref[...] = _rms_norm(x3, gf_ref[...]) if final_norm else x3


def _mix_ffn(x1, c, gl, ya, wpw, wout, g2, wup, wdn, gf, tm, final_norm):
    n_tok = x1.shape[0]
    row = lambda w: pl.BlockSpec((tm, w), lambda i: (i, 0))
    consts = (wpw, wout, g2, wup, wdn, gf)
    return pl.pallas_call(
        functools.partial(_mix_ffn_kernel, final_norm=final_norm),
        grid=(n_tok // tm,),
        in_specs=[row(D_MODEL), row(D_MODEL), row(2 * D_MODEL), row(D_MODEL)]
                 + [_const_spec(c.shape) for c in consts],
        out_specs=row(D_MODEL),
        out_shape=jax.ShapeDtypeStruct((n_tok, D_MODEL), F32),
        compiler_params=pltpu.CompilerParams(
            dimension_semantics=("arbitrary",), vmem_limit_bytes=VMEM_LIMIT_BYTES),
        name="mix_ffn",
    )(x1, c, gl, ya, *consts)


def _ssm_params(a_re, a_im, log_dt, b_re, b_im, c_re, c_im):
    dt = jnp.exp(log_dt)[:, None]
    mag = jnp.exp(dt * a_re)
    abar_re = mag * jnp.cos(dt * a_im)
    abar_im = mag * jnp.sin(dt * a_im)
    den = a_re * a_re + a_im * a_im
    nr = abar_re - 1.0
    ni = abar_im
    f_re = (nr * a_re + ni * a_im) / den
    f_im = (ni * a_re - nr * a_im) / den
    bb_re = f_re[..., None] * b_re - f_im[..., None] * b_im
    bb_im = f_re[..., None] * b_im + f_im[..., None] * b_re

    pair_eye = jnp.eye(PAIRS_PER_SG, dtype=F32)[:, None, :, None]
    block_mask = pair_eye * jnp.eye(2, dtype=F32)[None, :, None, :]

    bb = jnp.stack([bb_re, bb_im], axis=0)
    bb = bb.reshape(2, N_SG, PAIRS_PER_SG, 2, SSM_STATE, SSM_GROUP)
    bb = bb.transpose(1, 2, 3, 5, 0, 4)
    bw = (bb[:, :, :, :, None, :, None, :]
          * block_mask[None, :, :, None, :, None, :, None])
    bw = bw.reshape(N_SG, SG_IN, SG_STATE).astype(BF16)

    cc = jnp.stack([c_re, -c_im], axis=0)
    cc = cc.reshape(2, N_SG, PAIRS_PER_SG, 2, SSM_GROUP, SSM_STATE)
    cc = cc.transpose(1, 2, 0, 3, 5, 4)
    cw = (cc[:, :, :, :, :, None, None, :]
          * block_mask[None, :, None, :, None, :, :, None])
    cw = cw.reshape(N_SG, SG_STATE, MXU_DIM).astype(BF16)

    ar = abar_re.reshape(N_SG, PAIRS_PER_SG, LANES)
    ai = abar_im.reshape(N_SG, PAIRS_PER_SG, LANES)
    return bw, ar, ai, cw


def kernel(x, norm_ffn1_g, w_ffn1_up, w_ffn1_down, norm_mix_g, w_in, ssm_a_re, ssm_a_im, ssm_log_dt, ssm_b_re, ssm_b_im, ssm_c_re, ssm_c_im, ssm_d, w_ssm_glu, conv_dw_w, conv_dw_b, conv_ln_g, conv_ln_b, w_conv_pw, w_out, norm_ffn2_g, w_ffn2_up, w_ffn2_down, norm_final_g):
    bsz, seq, d = x.shape
    depth = w_in.shape[0]
    assert d == D_MODEL and w_ffn1_up.shape[-1] == 2 * D_FF
    tm = 256
    th = 256
    assert seq % tm == 0 and seq % (2 * th) == 0
    row = lambda a: a.reshape(1, -1).astype(F32)

    x2d = x.reshape(bsz * seq, d)
    for l in range(depth):
        conv_w = jnp.pad(conv_dw_w[l].reshape(CONV_WIDTH, d), ((0, 1), (0, 0)))
        x1, u, gl, c = _ffn_conv(
            x2d, row(norm_ffn1_g[l]), w_ffn1_up[l].astype(BF16), w_ffn1_down[l].astype(BF16),
            row(norm_mix_g[l]), w_in[l].astype(BF16), conv_w, row(conv_dw_b[l]),
            row(conv_ln_g[l]), row(conv_ln_b[l]), tm, seq // tm)
        bw, ar, ai, cw = _ssm_params(ssm_a_re[l], ssm_a_im[l], ssm_log_dt[l], ssm_b_re[l],
                                     ssm_b_im[l], ssm_c_re[l], ssm_c_im[l])
        ya = _ssm_glu(u.reshape(bsz, seq, d), bw, ar, ai, cw, row(ssm_d[l]),
                      w_ssm_glu[l].astype(BF16), th)
        x2d = _mix_ffn(x1, c, gl, ya.reshape(bsz * seq, d), w_conv_pw[l].astype(BF16),
                       w_out[l].astype(BF16), row(norm_ffn2_g[l]), w_ffn2_up[l].astype(BF16),
                       w_ffn2_down[l].astype(BF16), row(norm_final_g), tm,
                       final_norm=(l == depth - 1))
    return x2d.reshape(bsz, seq, d)
```

```python
import functools

import jax
import jax.numpy as jnp
from jax import lax
from jax.experimental import pallas as pl
from jax.experimental.pallas import tpu as pltpu

D_MODEL = 1024
D_FF = 2816
SSM_GROUP = 16
N_GROUPS = 64
SSM_STATE = 64
CONV_WIDTH = 31
RMS_EPS = 1e-6
LN_EPS = 1e-5
FFN_SCALE = 0.5

LANES = 128
SUBLANES = 8
MXU_DIM = 256
VMEM_LIMIT_BYTES = 56 * 1024 * 1024

PAIRS_PER_SG = SUBLANES
N_SG = N_GROUPS // (2 * PAIRS_PER_SG)
SG_IN = 2 * SSM_GROUP * PAIRS_PER_SG
SG_STATE = 2 * LANES * PAIRS_PER_SG
TOKEN_PITCH = 12
N_SLABS = D_MODEL // LANES
CONV_HALO = 32
CONV_CHAINS = 2

FFN_CHUNKS = ((0, 1536), (1536, D_FF))

BF16 = jnp.bfloat16
F32 = jnp.float32


def _dot(a, b):
    return jnp.dot(a, b, preferred_element_type=F32)


def _rms_norm(x, g):
    return x * lax.rsqrt(jnp.mean(x * x, axis=-1, keepdims=True) + RMS_EPS) * g


def _sigmoid(x):
    return 1.0 / (1.0 + jnp.exp(-x))


def _zero_of(x):
    bits = lax.bitcast_convert_type(x, jnp.uint32)
    return lax.bitcast_convert_type((bits >> 16) >> 16, F32)


def _swiglu(h, wup_ref, wdn_ref):
    acc = None
    for c0, c1 in FFN_CHUNKS:
        gate = _dot(h, wup_ref[:, c0:c1])
        up = _dot(h, wup_ref[:, D_FF + c0:D_FF + c1])
        act = (gate * _sigmoid(gate) * up).astype(BF16)
        part = _dot(act, wdn_ref[c0:c1, :])
        acc = part if acc is None else acc + part
    return acc


def _const_spec(shape):
    nd = len(shape)
    return pl.BlockSpec(shape, lambda *_: (0,) * nd, pipeline_mode=pl.Buffered(1))


def _ffn_conv_kernel(x_ref, g1_ref, wup_ref, wdn_ref, gm_ref, win_ref,
                     cw_ref, cb_ref, lng_ref, lnb_ref,
                     x1_ref, u_ref, gl_ref, c_ref, z_ref, cv_ref, *, tm, blocks_per_seq):
    g = pl.program_id(0)

    @pl.when(g == 0)
    def _():
        z_ref[...] = jnp.zeros_like(z_ref)

    base = CONV_HALO - (CONV_WIDTH - 1)
    tiles = [(c, r) for c in range(N_SLABS) for r in range(tm // SUBLANES)]
    tiles_per_chain = len(tiles) // CONV_CHAINS
    prev = [None] * CONV_CHAINS
    for i in range(tiles_per_chain):
        for ch in range(CONV_CHAINS):
            c, r = tiles[ch * tiles_per_chain + i]
            lanes = slice(c * LANES, (c + 1) * LANES)
            acc = jnp.broadcast_to(cb_ref[:, lanes], (SUBLANES, LANES))
            if prev[ch] is not None:
                acc = acc + _zero_of(prev[ch])
            for k in range(CONV_WIDTH):
                rows = pl.ds(base + k + r * SUBLANES, SUBLANES, stride=1)
                acc = acc + cw_ref[k:k + 1, lanes] * z_ref[c, rows, :]
            cv_ref[r * SUBLANES:(r + 1) * SUBLANES, lanes] = acc
            prev[ch] = acc

    cv = cv_ref[...]
    mu = jnp.mean(cv, axis=-1, keepdims=True)
    xc = cv - mu
    var = jnp.mean(xc * xc, axis=-1, keepdims=True)
    ln = xc * lax.rsqrt(var + LN_EPS) * lng_ref[...] + lnb_ref[...]
    c_ref[...] = (ln * _sigmoid(ln)).astype(c_ref.dtype)

    x = x_ref[...]
    h = _rms_norm(x, g1_ref[...]).astype(BF16)
    x1 = x + FFN_SCALE * _swiglu(h, wup_ref, wdn_ref)
    x1_ref[...] = x1
    h2 = _rms_norm(x1, gm_ref[...]).astype(BF16)
    u_ref[...] = _dot(h2, win_ref[:, 0:D_MODEL]).astype(u_ref.dtype)
    gl_ref[...] = _dot(h2, win_ref[:, 3 * D_MODEL:5 * D_MODEL]).astype(gl_ref.dtype)
    v = _dot(h2, win_ref[:, D_MODEL:3 * D_MODEL])
    z = v[:, :D_MODEL] * _sigmoid(v[:, D_MODEL:])

    seq_start = (g % blocks_per_seq) == 0
    for c in range(N_SLABS):
        tail = z_ref[c, tm:tm + CONV_HALO, :]
        z_ref[c, 0:CONV_HALO, :] = jnp.where(seq_start, 0.0, tail)
        z_ref[c, CONV_HALO:CONV_HALO + tm, :] = z[:, c * LANES:(c + 1) * LANES]


def _ffn_conv(x2d, g1, wup, wdn, gm, win, cw, cb, lng, lnb, tm, blocks_per_seq):
    n_tok = x2d.shape[0]
    n_blk = n_tok // tm
    cur = lambda w: pl.BlockSpec((tm, w), lambda i: (jnp.minimum(i, n_blk - 1), 0))
    prev = lambda w: pl.BlockSpec((tm, w), lambda i: (jnp.maximum(i - 1, 0), 0))
    consts = (g1, wup, wdn, gm, win, cw, cb, lng, lnb)
    return pl.pallas_call(
        functools.partial(_ffn_conv_kernel, tm=tm, blocks_per_seq=blocks_per_seq),
        grid=(n_blk + 1,),
        in_specs=[cur(D_MODEL)] + [_const_spec(c.shape) for c in consts],
        out_specs=[cur(D_MODEL), cur(D_MODEL), cur(2 * D_MODEL), prev(D_MODEL)],
        out_shape=[jax.ShapeDtypeStruct((n_tok, D_MODEL), F32),
                   jax.ShapeDtypeStruct((n_tok, D_MODEL), BF16),
                   jax.ShapeDtypeStruct((n_tok, 2 * D_MODEL), BF16),
                   jax.ShapeDtypeStruct((n_tok, D_MODEL), BF16)],
        scratch_shapes=[pltpu.VMEM((N_SLABS, CONV_HALO + tm, LANES), F32),
                        pltpu.VMEM((tm, D_MODEL), F32)],
        compiler_params=pltpu.CompilerParams(
            dimension_semantics=("arbitrary",), vmem_limit_bytes=VMEM_LIMIT_BYTES),
        name="ffn_conv",
    )(x2d, *consts)


def _bu_scatter(u_sg, bw_sg, re_ref, im_ref, sg, th):
    bu = _dot(u_sg, bw_sg)
    for j in range(PAIRS_PER_SG):
        c0 = 2 * LANES * j
        re_ref[sg, pl.ds(j, th, stride=TOKEN_PITCH), :] = bu[:, c0:c0 + LANES]
        im_ref[sg, pl.ds(j, th, stride=TOKEN_PITCH), :] = bu[:, c0 + LANES:c0 + 2 * LANES]


def _ssm_glu_kernel(u_ref, un_ref, bw_ref, ar_ref, ai_ref, cw_ref, d_ref, wglu_ref,
                    ya_ref, are_ref, aim_ref, bre_ref, bim_ref, carry_ref, *, th):
    scan_tokens = th // N_SG

    @pl.when(pl.program_id(1) == 0)
    def _():
        carry_ref[...] = jnp.zeros_like(carry_ref)
        for sg in range(N_SG):
            _bu_scatter(u_ref[0, 0:th, sg * SG_IN:(sg + 1) * SG_IN], bw_ref[sg],
                        are_ref, aim_ref, sg, th)

    ar = [ar_ref[sg] for sg in range(N_SG)]
    ai = [ai_ref[sg] for sg in range(N_SG)]

    def recur_and_project(cur_re, cur_im, nxt_re, nxt_im, load_next_u):
        def quarter(q, carry):
            row0 = q * (scan_tokens * TOKEN_PITCH)
            carry = list(carry)
            u_q = load_next_u(q)
            steps_per_pair = scan_tokens // PAIRS_PER_SG
            deps = []
            for j in range(PAIRS_PER_SG):
                if j >= 2:
                    carry[0] = carry[0] + _zero_of(deps[j - 2])
                for t in range(j * steps_per_pair, (j + 1) * steps_per_pair):
                    rows = pl.ds(row0 + t * TOKEN_PITCH, PAIRS_PER_SG, stride=1)
                    for sg in range(N_SG):
                        sr, si = carry[2 * sg], carry[2 * sg + 1]
                        nr = ar[sg] * sr - ai[sg] * si + cur_re[sg, rows, :]
                        ni = ar[sg] * si + ai[sg] * sr + cur_im[sg, rows, :]
                        cur_re[sg, rows, :] = nr
                        cur_im[sg, rows, :] = ni
                        carry[2 * sg], carry[2 * sg + 1] = nr, ni
                bu = _dot(u_q, bw_ref[q, :, 2 * LANES * j:2 * LANES * (j + 1)])
                deps.append(bu[0:SUBLANES, 0:LANES])
                nxt_re[q, pl.ds(j, th, stride=TOKEN_PITCH), :] = bu[:, :LANES]
                nxt_im[q, pl.ds(j, th, stride=TOKEN_PITCH), :] = bu[:, LANES:]
            return tuple(carry)

        init = tuple(carry_ref[i] for i in range(2 * N_SG))
        final = init
        for q in range(N_SG):
            final = quarter(q, final)
        for i in range(2 * N_SG):
            carry_ref[i] = final[i]

    def output(s_re, s_im, u_half):
        ys = []
        for sg in range(N_SG):
            parts = []
            for j in range(PAIRS_PER_SG):
                parts += [s_re[sg, pl.ds(j, th, stride=TOKEN_PITCH), :].astype(BF16),
                          s_im[sg, pl.ds(j, th, stride=TOKEN_PITCH), :].astype(BF16)]
            ys.append(_dot(jnp.concatenate(parts, axis=1), cw_ref[sg]))
        y = jnp.concatenate(ys, axis=1) + d_ref[...] * u_half.astype(F32)
        glu = _dot(jax.nn.gelu(y).astype(BF16), wglu_ref[...])
        return (glu[:, :D_MODEL] * _sigmoid(glu[:, D_MODEL:])).astype(ya_ref.dtype)

    def u_second(q):
        return u_ref[0, th:2 * th, q * SG_IN:(q + 1) * SG_IN]

    def u_next(q):
        return un_ref[0, :, q * SG_IN:(q + 1) * SG_IN]

    recur_and_project(are_ref, aim_ref, bre_ref, bim_ref, u_second)
    ya_ref[0, 0:th, :] = output(are_ref, aim_ref, u_ref[0, 0:th, :])
    recur_and_project(bre_ref, bim_ref, are_ref, aim_ref, u_next)
    ya_ref[0, th:2 * th, :] = output(bre_ref, bim_ref, u_ref[0, th:2 * th, :])


def _ssm_glu(u3d, bw, ar, ai, cw, d, wglu, th):
    bsz, seq, _ = u3d.shape
    n_half = seq // th
    consts = (bw, ar, ai, cw, d, wglu)
    state = pltpu.VMEM((N_SG, th * TOKEN_PITCH, LANES), F32)
    return pl.pallas_call(
        functools.partial(_ssm_glu_kernel, th=th),
        grid=(bsz, n_half // 2),
        in_specs=[pl.BlockSpec((1, 2 * th, D_MODEL), lambda b, i: (b, i, 0)),
                  pl.BlockSpec((1, th, D_MODEL),
                               lambda b, i: (b, jnp.minimum(2 * i + 2, n_half - 1), 0))]
                 + [_const_spec(c.shape) for c in consts],
        out_specs=pl.BlockSpec((1, 2 * th, D_MODEL), lambda b, i: (b, i, 0)),
        out_shape=jax.ShapeDtypeStruct((bsz, seq, D_MODEL), BF16),
        scratch_shapes=[state, state, state, state,
                        pltpu.VMEM((2 * N_SG, SUBLANES, LANES), F32)],
        compiler_params=pltpu.CompilerParams(
            dimension_semantics=("arbitrary", "arbitrary"),
            vmem_limit_bytes=VMEM_LIMIT_BYTES),
        name="ssm_glu",
    )(u3d, u3d, *consts)


def _mix_ffn_kernel(x1_ref, c_ref, gl_ref, ya_ref, wpw_ref, wout_ref, g2_ref, wup_ref,
                    wdn_ref, gf_ref, o_ref, *, final_norm):
    yb = _dot(c_ref[...], wpw_ref[...])
    gates = _sigmoid(gl_ref[...].astype(F32))
    mix = gates[:, :D_MODEL] * ya_ref[...].astype(F32) + gates[:, D_MODEL:] * yb
    x2 = x1_ref[...] + _dot(mix.astype(BF16), wout_ref[...])
    h = _rms_norm(x2, g2_ref[...]).astype(BF16)
    x3 = x2 + FFN_SCALE * _swiglu(h, wup_ref, wdn_ref)
    o_ref[...] = _rms_norm(x3, gf_ref[...]) if final_norm else x3


def _mix_ffn(x1, c, gl, ya, wpw, wout, g2, wup, wdn, gf, tm, final_norm):
    n_tok = x1.shape[0]
    row = lambda w: pl.BlockSpec((tm, w), lambda i: (i, 0))
    consts = (wpw, wout, g2, wup, wdn, gf)
    return pl.pallas_call(
        functools.partial(_mix_ffn_kernel, final_norm=final_norm),
        grid=(n_tok // tm,),
        in_specs=[row(D_MODEL), row(D_MODEL), row(2 * D_MODEL), row(D_MODEL)]
                 + [_const_spec(c.shape) for c in consts],
        out_specs=row(D_MODEL),
        out_shape=jax.ShapeDtypeStruct((n_tok, D_MODEL), F32),
        compiler_params=pltpu.CompilerParams(
            dimension_semantics=("arbitrary",), vmem_limit_bytes=VMEM_LIMIT_BYTES),
        name="mix_ffn",
    )(x1, c, gl, ya, *consts)


def _ssm_params(a_re, a_im, log_dt, b_re, b_im, c_re, c_im):
    dt = jnp.exp(log_dt)[:, None]
    mag = jnp.exp(dt * a_re)
    abar_re = mag * jnp.cos(dt * a_im)
    abar_im = mag * jnp.sin(dt * a_im)
    den = a_re * a_re + a_im * a_im
    nr = abar_re - 1.0
    ni = abar_im
    f_re = (nr * a_re + ni * a_im) / den
    f_im = (ni * a_re - nr * a_im) / den
    bb_re = f_re[..., None] * b_re - f_im[..., None] * b_im
    bb_im = f_re[..., None] * b_im + f_im[..., None] * b_re

    pair_eye = jnp.eye(PAIRS_PER_SG, dtype=F32)

    def in_pair_lanes(a):
        zeros = jnp.zeros_like(a[..., 0:1, :, :, :])
        first = jnp.concatenate([a[..., 0:1, :, :, :], zeros], axis=-1)
        second = jnp.concatenate([zeros, a[..., 1:2, :, :, :]], axis=-1)
        return jnp.concatenate([first, second], axis=-4)

    bb = jnp.stack([bb_re, bb_im], axis=0)
    bb = bb.reshape(2, N_SG, PAIRS_PER_SG, 2, SSM_STATE, SSM_GROUP)
    bb = in_pair_lanes(bb.transpose(1, 2, 3, 5, 0, 4))
    bw = bb[:, :, :, :, None, :, :] * pair_eye[None, :, None, None, :, None, None]
    bw = bw.reshape(N_SG, SG_IN, SG_STATE).astype(BF16)

    cc = jnp.stack([c_re, -c_im], axis=0)
    cc = cc.reshape(2, N_SG, PAIRS_PER_SG, 2, SSM_GROUP, SSM_STATE)
    cc = in_pair_lanes(cc.transpose(1, 2, 3, 4, 0, 5))
    cwt = cc[:, :, :, :, None, :, :] * pair_eye[None, :, None, None, :, None, None]
    cw = jnp.swapaxes(cwt.reshape(N_SG, MXU_DIM, SG_STATE).astype(BF16), 1, 2)

    ar = abar_re.reshape(N_SG, PAIRS_PER_SG, LANES)
    ai = abar_im.reshape(N_SG, PAIRS_PER_SG, LANES)
    return bw, ar, ai, cw


def kernel(x, norm_ffn1_g, w_ffn1_up, w_ffn1_down, norm_mix_g, w_in, ssm_a_re, ssm_a_im, ssm_log_dt, ssm_b_re, ssm_b_im, ssm_c_re, ssm_c_im, ssm_d, w_ssm_glu, conv_dw_w, conv_dw_b, conv_ln_g, conv_ln_b, w_conv_pw, w_out, norm_ffn2_g, w_ffn2_up, w_ffn2_down, norm_final_g):
    bsz, seq, d = x.shape
    depth = w_in.shape[0]
    assert d == D_MODEL and w_ffn1_up.shape[-1] == 2 * D_FF
    tm = 256
    th = 256
    assert seq % tm == 0 and seq % (2 * th) == 0
    row = lambda a: a.reshape(1, -1).astype(F32)

    x2d = x.reshape(bsz * seq, d)
    for l in range(depth):
        conv_w = jnp.pad(conv_dw_w[l].reshape(CONV_WIDTH, d), ((0, 1), (0, 0)))
        x1, u, gl, c = _ffn_conv(
            x2d, row(norm_ffn1_g[l]), w_ffn1_up[l].astype(BF16), w_ffn1_down[l].astype(BF16),
            row(norm_mix_g[l]), w_in[l].astype(BF16), conv_w, row(conv_dw_b[l]),
            row(conv_ln_g[l]), row(conv_ln_b[l]), tm, seq // tm)
        bw, ar, ai, cw = _ssm_params(ssm_a_re[l], ssm_a_im[l], ssm_log_dt[l], ssm_b_re[l],
                                     ssm_b_im[l], ssm_c_re[l], ssm_c_im[l])
        ya = _ssm_glu(u.reshape(bsz, seq, d), bw, ar, ai, cw, row(ssm_d[l]),
                      w_ssm_glu[l].astype(BF16), th)
        x2d = _mix_ffn(x1, c, gl, ya.reshape(bsz * seq, d), w_conv_pw[l].astype(BF16),
                       w_out[l].astype(BF16), row(norm_ffn2_g[l]), w_ffn2_up[l].astype(BF16),
                       w_ffn2_down[l].astype(BF16), row(norm_final_g), tm,
                       final_norm=(l == depth - 1))
    return x2d.reshape(bsz, seq, d)
```

```python
import functools

import jax
import jax.numpy as jnp
from jax import lax
from jax.experimental import pallas as pl
from jax.experimental.pallas import tpu as pltpu

D_MODEL = 1024
D_FF = 2816
SSM_GROUP = 16
N_GROUPS = 64
SSM_STATE = 64
CONV_WIDTH = 31
RMS_EPS = 1e-6
LN_EPS = 1e-5
FFN_SCALE = 0.5

LANES = 128
SUBLANES = 8
MXU_DIM = 256
VMEM_LIMIT_BYTES = 56 * 1024 * 1024

PAIRS_PER_SG = SUBLANES
N_SG = N_GROUPS // (2 * PAIRS_PER_SG)
SG_IN = 2 * SSM_GROUP * PAIRS_PER_SG
SG_STATE = 2 * LANES * PAIRS_PER_SG
TOKEN_PITCH = 12
N_SLABS = D_MODEL // LANES
CONV_HALO = 32
CONV_CHAINS = 2

FFN_CHUNKS = ((0, 1536), (1536, D_FF))

BF16 = jnp.bfloat16
F32 = jnp.float32


def _dot(a, b):
    return jnp.dot(a, b, preferred_element_type=F32)


def _rms_norm(x, g):
    return x * lax.rsqrt(jnp.mean(x * x, axis=-1, keepdims=True) + RMS_EPS) * g


def _sigmoid(x):
    return 1.0 / (1.0 + jnp.exp(-x))


def _zero_of(x):
    bits = lax.bitcast_convert_type(x, jnp.uint32)
    return lax.bitcast_convert_type((bits >> 16) >> 16, F32)


def _swiglu(h, wup_ref, wdn_ref):
    acc = None
    for c0, c1 in FFN_CHUNKS:
        gate = _dot(h, wup_ref[:, c0:c1])
        up = _dot(h, wup_ref[:, D_FF + c0:D_FF + c1])
        act = (gate * _sigmoid(gate) * up).astype(BF16)
        part = _dot(act, wdn_ref[c0:c1, :])
        acc = part if acc is None else acc + part
    return acc


def _const_spec(shape):
    nd = len(shape)
    return pl.BlockSpec(shape, lambda *_: (0,) * nd, pipeline_mode=pl.Buffered(1))


def _ffn_conv_kernel(x_ref, g1_ref, wup_ref, wdn_ref, gm_ref, win_ref,
                     cw_ref, cb_ref, lng_ref, lnb_ref,
                     x1_ref, u_ref, gl_ref, c_ref, z_ref, cv_ref, *, tm, blocks_per_seq):
    g = pl.program_id(0)

    @pl.when(g == 0)
    def _():
        z_ref[...] = jnp.zeros_like(z_ref)

    base = CONV_HALO - (CONV_WIDTH - 1)
    tiles = [(c, r) for c in range(N_SLABS) for r in range(tm // SUBLANES)]
    tiles_per_chain = len(tiles) // CONV_CHAINS
    prev = [None] * CONV_CHAINS
    for i in range(tiles_per_chain):
        for ch in range(CONV_CHAINS):
            c, r = tiles[ch * tiles_per_chain + i]
            lanes = slice(c * LANES, (c + 1) * LANES)
            acc = jnp.broadcast_to(cb_ref[:, lanes], (SUBLANES, LANES))
            if prev[ch] is not None:
                acc = acc + _zero_of(prev[ch])
            for k in range(CONV_WIDTH):
                rows = pl.ds(base + k + r * SUBLANES, SUBLANES, stride=1)
                acc = acc + cw_ref[k:k + 1, lanes] * z_ref[c, rows, :]
            cv_ref[r * SUBLANES:(r + 1) * SUBLANES, lanes] = acc
            prev[ch] = acc

    cv = cv_ref[...]
    mu = jnp.mean(cv, axis=-1, keepdims=True)
    xc = cv - mu
    var = jnp.mean(xc * xc, axis=-1, keepdims=True)
    ln = xc * lax.rsqrt(var + LN_EPS) * lng_ref[...] + lnb_ref[...]
    c_ref[...] = (ln * _sigmoid(ln)).astype(c_ref.dtype)

    x = x_ref[...]
    h = _rms_norm(x, g1_ref[...]).astype(BF16)
    x1 = x + FFN_SCALE * _swiglu(h, wup_ref, wdn_ref)
    x1_ref[...] = x1
    h2 = _rms_norm(x1, gm_ref[...]).astype(BF16)
    u_ref[...] = _dot(h2, win_ref[:, 0:D_MODEL]).astype(u_ref.dtype)
    gl_ref[...] = _dot(h2, win_ref[:, 3 * D_MODEL:5 * D_MODEL]).astype(gl_ref.dtype)
    v = _dot(h2, win_ref[:, D_MODEL:3 * D_MODEL])
    z = v[:, :D_MODEL] * _sigmoid(v[:, D_MODEL:])

    seq_start = (g % blocks_per_seq) == 0
    for c in range(N_SLABS):
        tail = z_ref[c, tm:tm + CONV_HALO, :]
        z_ref[c, 0:CONV_HALO, :] = jnp.where(seq_start, 0.0, tail)
        z_ref[c, CONV_HALO:CONV_HALO + tm, :] = z[:, c * LANES:(c + 1) * LANES]


def _ffn_conv(x2d, g1, wup, wdn, gm, win, cw, cb, lng, lnb, tm, blocks_per_seq):
    n_tok = x2d.shape[0]
    n_blk = n_tok // tm
    cur = lambda w: pl.BlockSpec((tm, w), lambda i: (jnp.minimum(i, n_blk - 1), 0))
    prev = lambda w: pl.BlockSpec((tm, w), lambda i: (jnp.maximum(i - 1, 0), 0))
    consts = (g1, wup, wdn, gm, win, cw, cb, lng, lnb)
    return pl.pallas_call(
        functools.partial(_ffn_conv_kernel, tm=tm, blocks_per_seq=blocks_per_seq),
        grid=(n_blk + 1,),
        in_specs=[cur(D_MODEL)] + [_const_spec(c.shape) for c in consts],
        out_specs=[cur(D_MODEL), cur(D_MODEL), cur(2 * D_MODEL), prev(D_MODEL)],
        out_shape=[jax.ShapeDtypeStruct((n_tok, D_MODEL), F32),
                   jax.ShapeDtypeStruct((n_tok, D_MODEL), BF16),
                   jax.ShapeDtypeStruct((n_tok, 2 * D_MODEL), BF16),
                   jax.ShapeDtypeStruct((n_tok, D_MODEL), BF16)],
        scratch_shapes=[pltpu.VMEM((N_SLABS, CONV_HALO + tm, LANES), F32),
                        pltpu.VMEM((tm, D_MODEL), F32)],
        compiler_params=pltpu.CompilerParams(
            dimension_semantics=("arbitrary",), vmem_limit_bytes=VMEM_LIMIT_BYTES),
        name="ffn_conv",
    )(x2d, *consts)


def _bu_scatter(u_sg, bw_sg, re_ref, im_ref, sg, th):
    bu = _dot(u_sg, bw_sg)
    for j in range(PAIRS_PER_SG):
        c0 = 2 * LANES * j
        re_ref[sg, pl.ds(j, th, stride=TOKEN_PITCH), :] = bu[:, c0:c0 + LANES]
        im_ref[sg, pl.ds(j, th, stride=TOKEN_PITCH), :] = bu[:, c0 + LANES:c0 + 2 * LANES]


def _ssm_glu_kernel(u_ref, un_ref, bw_ref, ar_ref, ai_ref, cw_ref, d_ref, wglu_ref,
                    ya_ref, are_ref, aim_ref, bre_ref, bim_ref, carry_ref, *, th):
    scan_tokens = th // N_SG

    @pl.when(pl.program_id(1) == 0)
    def _():
        carry_ref[...] = jnp.zeros_like(carry_ref)
        for sg in range(N_SG):
            _bu_scatter(u_ref[0, 0:th, sg * SG_IN:(sg + 1) * SG_IN], bw_ref[sg],
                        are_ref, aim_ref, sg, th)

    ar = [ar_ref[sg] for sg in range(N_SG)]
    ai = [ai_ref[sg] for sg in range(N_SG)]

    def recur_and_project(cur_re, cur_im, nxt_re, nxt_im, load_next_u):
        def quarter(q, carry):
            row0 = q * (scan_tokens * TOKEN_PITCH)
            carry = list(carry)
            u_q = load_next_u(q)
            steps_per_pair = scan_tokens // PAIRS_PER_SG
            deps = []
            for j in range(PAIRS_PER_SG):
                if j >= 2:
                    carry[0] = carry[0] + _zero_of(deps[j - 2])
                for t in range(j * steps_per_pair, (j + 1) * steps_per_pair):
                    rows = pl.ds(row0 + t * TOKEN_PITCH, PAIRS_PER_SG, stride=1)
                    for sg in range(N_SG):
                        sr, si = carry[2 * sg], carry[2 * sg + 1]
                        nr = ar[sg] * sr - ai[sg] * si + cur_re[sg, rows, :]
                        ni = ar[sg] * si + ai[sg] * sr + cur_im[sg, rows, :]
                        cur_re[sg, rows, :] = nr
                        cur_im[sg, rows, :] = ni
                        carry[2 * sg], carry[2 * sg + 1] = nr, ni
                bu = _dot(u_q, bw_ref[q, :, 2 * LANES * j:2 * LANES * (j + 1)])
                deps.append(bu[0:SUBLANES, 0:LANES])
                nxt_re[q, pl.ds(j, th, stride=TOKEN_PITCH), :] = bu[:, :LANES]
                nxt_im[q, pl.ds(j, th, stride=TOKEN_PITCH), :] = bu[:, LANES:]
            return tuple(carry)

        init = tuple(carry_ref[i] for i in range(2 * N_SG))
        final = init
        for q in range(N_SG):
            final = quarter(q, final)
        for i in range(2 * N_SG):
            carry_ref[i] = final[i]

    def output(s_re, s_im, u_half):
        ys = []
        for sg in range(N_SG):
            parts = []
            for j in range(PAIRS_PER_SG):
                parts += [s_re[sg, pl.ds(j, th, stride=TOKEN_PITCH), :].astype(BF16),
                          s_im[sg, pl.ds(j, th, stride=TOKEN_PITCH), :].astype(BF16)]
            ys.append(_dot(jnp.concatenate(parts, axis=1), cw_ref[sg]))
        y = jnp.concatenate(ys, axis=1) + d_ref[...] * u_half.astype(F32)
        glu = _dot(jax.nn.gelu(y).astype(BF16), wglu_ref[...])
        return (glu[:, :D_MODEL] * _sigmoid(glu[:, D_MODEL:])).astype(ya_ref.dtype)

    def u_second(q):
        return u_ref[0, th:2 * th, q * SG_IN:(q + 1) * SG_IN]

    def u_next(q):
        return un_ref[0, :, q * SG_IN:(q + 1) * SG_IN]

    recur_and_project(are_ref, aim_ref, bre_ref, bim_ref, u_second)
    ya_ref[0, 0:th, :] = output(are_ref, aim_ref, u_ref[0, 0:th, :])
    recur_and_project(bre_ref, bim_ref, are_ref, aim_ref, u_next)
    ya_ref[0, th:2 * th, :] = output(bre_ref, bim_ref, u_ref[0, th:2 * th, :])


def _ssm_glu(u3d, bw, ar, ai, cw, d, wglu, th):
    bsz, seq, _ = u3d.shape
    n_half = seq // th
    consts = (bw, ar, ai, cw, d, wglu)
    state = pltpu.VMEM((N_SG, th * TOKEN_PITCH, LANES), F32)
    return pl.pallas_call(
        functools.partial(_ssm_glu_kernel, th=th),
        grid=(bsz, n_half // 2),
        in_specs=[pl.BlockSpec((1, 2 * th, D_MODEL), lambda b, i: (b, i, 0)),
                  pl.BlockSpec((1, th, D_MODEL),
                               lambda b, i: (b, jnp.minimum(2 * i + 2, n_half - 1), 0))]
                 + [_const_spec(c.shape) for c in consts],
        out_specs=pl.BlockSpec((1, 2 * th, D_MODEL), lambda b, i: (b, i, 0)),
        out_shape=jax.ShapeDtypeStruct((bsz, seq, D_MODEL), BF16),
        scratch_shapes=[state, state, state, state,
                        pltpu.VMEM((2 * N_SG, SUBLANES, LANES), F32)],
        compiler_params=pltpu.CompilerParams(
            dimension_semantics=("arbitrary", "arbitrary"),
            vmem_limit_bytes=VMEM_LIMIT_BYTES),
        name="ssm_glu",
    )(u3d, u3d, *consts)


def _mix_ffn_kernel(x1_ref, c_ref, gl_ref, ya_ref, wpw_ref, wout_ref, g2_ref, wup_ref,
                    wdn_ref, gf_ref, o_ref, *, final_norm):
    yb = _dot(c_ref[...], wpw_ref[...])
    gates = _sigmoid(gl_ref[...].astype(F32))
    mix = gates[:, :D_MODEL] * ya_ref[...].astype(F32) + gates[:, D_MODEL:] * yb
    x2 = x1_ref[...] + _dot(mix.astype(BF16), wout_ref[...])
    h = _rms_norm(x2, g2_ref[...]).astype(BF16)
    x3 = x2 + FFN_SCALE * _swiglu(h, wup_ref, wdn_ref)
    o_ref[...] = _rms_norm(x3, gf_ref[...]) if final_norm else x3


def _mix_ffn(x1, c, gl, ya, wpw, wout, g2, wup, wdn, gf, tm, final_norm):
    n_tok = x1.shape[0]
    row = lambda w: pl.BlockSpec((tm, w), lambda i: (i, 0))
    consts = (wpw, wout, g2, wup, wdn, gf)
    return pl.pallas_call(
        functools.partial(_mix_ffn_kernel, final_norm=final_norm),
        grid=(n_tok // tm,),
        in_specs=[row(D_MODEL), row(D_MODEL), row(2 * D_MODEL), row(D_MODEL)]
                 + [_const_spec(c.shape) for c in consts],
        out_specs=row(D_MODEL),
        out_shape=jax.ShapeDtypeStruct((n_tok, D_MODEL), F32),
        compiler_params=pltpu.CompilerParams(
            dimension_semantics=("arbitrary",), vmem_limit_bytes=VMEM_LIMIT_BYTES),
        name="mix_ffn",
    )(x1, c, gl, ya, *consts)


def _ssm_params(a_re, a_im, log_dt, b_re, b_im, c_re, c_im):
    dt = jnp.exp(log_dt)[:, None]
    mag = jnp.exp(dt * a_re)
    abar_re = mag * jnp.cos(dt * a_im)
    abar_im = mag * jnp.sin(dt * a_im)
    den = a_re * a_re + a_im * a_im
    nr = abar_re - 1.0
    ni = abar_im
    f_re = (nr * a_re + ni * a_im) / den
    f_im = (ni * a_re - nr * a_im) / den
    bb_re = f_re[..., None] * b_re - f_im[..., None] * b_im
    bb_im = f_re[..., None] * b_im + f_im[..., None] * b_re

    pair_block = (jnp.arange(SG_IN)[:, None] // (2 * SSM_GROUP)
                  == jnp.arange(SG_STATE)[None, :] // (2 * LANES)).astype(F32)

    def block_diagonal(a):
        tiled = jnp.broadcast_to(a[:, :, None, :], (N_SG, SG_IN, PAIRS_PER_SG, 2 * LANES))
        return (tiled.reshape(N_SG, SG_IN, SG_STATE) * pair_block).astype(BF16)

    def in_pair_lanes(a):
        zeros = jnp.zeros_like(a[..., 0:1, :, :, :])
        first = jnp.concatenate([a[..., 0:1, :, :, :], zeros], axis=-1)
        second = jnp.concatenate([zeros, a[..., 1:2, :, :, :]], axis=-1)
        return jnp.concatenate([first, second], axis=-4)

    bb = jnp.stack([bb_re, bb_im], axis=0)
    bb = bb.reshape(2, N_SG, PAIRS_PER_SG, 2, SSM_STATE, SSM_GROUP)
    bb = in_pair_lanes(bb.transpose(1, 2, 3, 5, 0, 4))
    bw = block_diagonal(bb.reshape(N_SG, SG_IN, 2 * LANES))

    cc = jnp.stack([c_re, -c_im], axis=0)
    cc = cc.reshape(2, N_SG, PAIRS_PER_SG, 2, SSM_GROUP, SSM_STATE)
    cc = in_pair_lanes(cc.transpose(1, 2, 3, 4, 0, 5))
    cw = jnp.swapaxes(block_diagonal(cc.reshape(N_SG, MXU_DIM, 2 * LANES)), 1, 2)

    ar = abar_re.reshape(N_SG, PAIRS_PER_SG, LANES)
    ai = abar_im.reshape(N_SG, PAIRS_PER_SG, LANES)
    return bw, ar, ai, cw


def kernel(x, norm_ffn1_g, w_ffn1_up, w_ffn1_down, norm_mix_g, w_in, ssm_a_re, ssm_a_im, ssm_log_dt, ssm_b_re, ssm_b_im, ssm_c_re, ssm_c_im, ssm_d, w_ssm_glu, conv_dw_w, conv_dw_b, conv_ln_g, conv_ln_b, w_conv_pw, w_out, norm_ffn2_g, w_ffn2_up, w_ffn2_down, norm_final_g):
    bsz, seq, d = x.shape
    depth = w_in.shape[0]
    assert d == D_MODEL and w_ffn1_up.shape[-1] == 2 * D_FF
    tm = 256
    th = 256
    tm_mix = 512
    assert seq % tm == 0 and seq % (2 * th) == 0 and (bsz * seq) % tm_mix == 0
    row = lambda a: a.reshape(1, -1).astype(F32)

    x2d = x.reshape(bsz * seq, d)
    for l in range(depth):
        conv_w = jnp.pad(conv_dw_w[l].reshape(CONV_WIDTH, d), ((0, 1), (0, 0)))
        x1, u, gl, c = _ffn_conv(
            x2d, row(norm_ffn1_g[l]), w_ffn1_up[l].astype(BF16), w_ffn1_down[l].astype(BF16),
            row(norm_mix_g[l]), w_in[l].astype(BF16), conv_w, row(conv_dw_b[l]),
            row(conv_ln_g[l]), row(conv_ln_b[l]), tm, seq // tm)
        bw, ar, ai, cw = _ssm_params(ssm_a_re[l], ssm_a_im[l], ssm_log_dt[l], ssm_b_re[l],
                                     ssm_b_im[l], ssm_c_re[l], ssm_c_im[l])
        ya = _ssm_glu(u.reshape(bsz, seq, d), bw, ar, ai, cw, row(ssm_d[l]),
                      w_ssm_glu[l].astype(BF16), th)
        x2d = _mix_ffn(x1, c, gl, ya.reshape(bsz * seq, d), w_conv_pw[l].astype(BF16),
                       w_out[l].astype(BF16), row(norm_ffn2_g[l]), w_ffn2_up[l].astype(BF16),
                       w_ffn2_down[l].astype(BF16), row(norm_final_g), tm_mix,
                       final_norm=(l == depth - 1))
    return x2d.reshape(bsz, seq, d)
```

```python
import functools

import jax
import jax.numpy as jnp
from jax import lax
from jax.experimental import pallas as pl
from jax.experimental.pallas import tpu as pltpu

D_MODEL = 1024
D_FF = 2816
SSM_GROUP = 16
N_GROUPS = 64
SSM_STATE = 64
CONV_WIDTH = 31
RMS_EPS = 1e-6
LN_EPS = 1e-5
FFN_SCALE = 0.5

LANES = 128
SUBLANES = 8
MXU_DIM = 256
VMEM_LIMIT_BYTES = 56 * 1024 * 1024

PAIRS_PER_SG = SUBLANES
N_SG = N_GROUPS // (2 * PAIRS_PER_SG)
SSM_CHUNK = 4
N_LANE_BLOCKS = D_MODEL // LANES
PAIRS_PER_BLOCK = LANES // (2 * SSM_GROUP)
BLOCK_STATE = 2 * LANES * PAIRS_PER_BLOCK
STATE_PITCH = 12
N_SLABS = D_MODEL // LANES
CONV_HALO = 32
CONV_CHAINS = 2

FFN_CHUNKS = ((0, 1536), (1536, D_FF))

BF16 = jnp.bfloat16
F32 = jnp.float32


def _dot(a, b):
    return jnp.dot(a, b, preferred_element_type=F32)


def _rms_norm(x, g):
    return x * lax.rsqrt(jnp.mean(x * x, axis=-1, keepdims=True) + RMS_EPS) * g


def _sigmoid(x):
    return 1.0 / (1.0 + jnp.exp(-x))


def _zero_of(x):
    bits = lax.bitcast_convert_type(x, jnp.uint32)
    return lax.bitcast_convert_type((bits >> 16) >> 16, F32)


def _swiglu(h, wup_ref, wdn_ref):
    acc = None
    for c0, c1 in FFN_CHUNKS:
        gate = _dot(h, wup_ref[:, c0:c1])
        up = _dot(h, wup_ref[:, D_FF + c0:D_FF + c1])
        act = (gate * _sigmoid(gate) * up).astype(BF16)
        part = _dot(act, wdn_ref[c0:c1, :])
        acc = part if acc is None else acc + part
    return acc


def _const_spec(shape):
    nd = len(shape)
    return pl.BlockSpec(shape, lambda *_: (0,) * nd, pipeline_mode=pl.Buffered(1))


def _ffn_conv_kernel(x_ref, g1_ref, wup_ref, wdn_ref, gm_ref, win_ref,
                     cw_ref, cb_ref, lng_ref, lnb_ref,
                     x1_ref, u_ref, gl_ref, c_ref, z_ref, cv_ref, *, tm, blocks_per_seq):
    g = pl.program_id(0)

    @pl.when(g == 0)
    def _():
        z_ref[...] = jnp.zeros_like(z_ref)

    base = CONV_HALO - (CONV_WIDTH - 1)
    tiles = [(c, r) for c in range(N_SLABS) for r in range(tm // SUBLANES)]
    tiles_per_chain = len(tiles) // CONV_CHAINS
    prev = [None] * CONV_CHAINS
    for i in range(tiles_per_chain):
        for ch in range(CONV_CHAINS):
            c, r = tiles[ch * tiles_per_chain + i]
            lanes = slice(c * LANES, (c + 1) * LANES)
            acc = jnp.broadcast_to(cb_ref[:, lanes], (SUBLANES, LANES))
            if prev[ch] is not None:
                acc = acc + _zero_of(prev[ch])
            for k in range(CONV_WIDTH):
                rows = pl.ds(base + k + r * SUBLANES, SUBLANES, stride=1)
                acc = acc + cw_ref[k:k + 1, lanes] * z_ref[c, rows, :]
            cv_ref[r * SUBLANES:(r + 1) * SUBLANES, lanes] = acc
            prev[ch] = acc

    cv = cv_ref[...]
    mu = jnp.mean(cv, axis=-1, keepdims=True)
    xc = cv - mu
    var = jnp.mean(xc * xc, axis=-1, keepdims=True)
    ln = xc * lax.rsqrt(var + LN_EPS) * lng_ref[...] + lnb_ref[...]
    c_ref[...] = (ln * _sigmoid(ln)).astype(c_ref.dtype)

    x = x_ref[...]
    h = _rms_norm(x, g1_ref[...]).astype(BF16)
    x1 = x + FFN_SCALE * _swiglu(h, wup_ref, wdn_ref)
    x1_ref[...] = x1
    h2 = _rms_norm(x1, gm_ref[...]).astype(BF16)
    u_ref[...] = _dot(h2, win_ref[:, 0:D_MODEL]).astype(u_ref.dtype)
    gl_ref[...] = _dot(h2, win_ref[:, 3 * D_MODEL:5 * D_MODEL]).astype(gl_ref.dtype)
    v = _dot(h2, win_ref[:, D_MODEL:3 * D_MODEL])
    z = v[:, :D_MODEL] * _sigmoid(v[:, D_MODEL:])

    seq_start = (g % blocks_per_seq) == 0
    for c in range(N_SLABS):
        tail = z_ref[c, tm:tm + CONV_HALO, :]
        z_ref[c, 0:CONV_HALO, :] = jnp.where(seq_start, 0.0, tail)
        z_ref[c, CONV_HALO:CONV_HALO + tm, :] = z[:, c * LANES:(c + 1) * LANES]


def _ffn_conv(x2d, g1, wup, wdn, gm, win, cw, cb, lng, lnb, tm, blocks_per_seq):
    n_tok = x2d.shape[0]
    n_blk = n_tok // tm
    cur = lambda w: pl.BlockSpec((tm, w), lambda i: (jnp.minimum(i, n_blk - 1), 0))
    prev = lambda w: pl.BlockSpec((tm, w), lambda i: (jnp.maximum(i - 1, 0), 0))
    consts = (g1, wup, wdn, gm, win, cw, cb, lng, lnb)
    return pl.pallas_call(
        functools.partial(_ffn_conv_kernel, tm=tm, blocks_per_seq=blocks_per_seq),
        grid=(n_blk + 1,),
        in_specs=[cur(D_MODEL)] + [_const_spec(c.shape) for c in consts],
        out_specs=[cur(D_MODEL), cur(D_MODEL), cur(2 * D_MODEL), prev(D_MODEL)],
        out_shape=[jax.ShapeDtypeStruct((n_tok, D_MODEL), F32),
                   jax.ShapeDtypeStruct((n_tok, D_MODEL), BF16),
                   jax.ShapeDtypeStruct((n_tok, 2 * D_MODEL), BF16),
                   jax.ShapeDtypeStruct((n_tok, D_MODEL), BF16)],
        scratch_shapes=[pltpu.VMEM((N_SLABS, CONV_HALO + tm, LANES), F32),
                        pltpu.VMEM((tm, D_MODEL), F32)],
        compiler_params=pltpu.CompilerParams(
            dimension_semantics=("arbitrary",), vmem_limit_bytes=VMEM_LIMIT_BYTES),
        name="ffn_conv",
    )(x2d, *consts)


def _ssm_chunk_kernel(u_ref, wb_ref, wt_ref, wc_ref, ar_ref, ai_ref, d_ref,
                      y_ref, sre_ref, sim_ref, yt_ref, carry_ref, *, nc):
    @pl.when(pl.program_id(1) == 0)
    def _():
        carry_ref[...] = jnp.zeros_like(carry_ref)

    def chunk_lanes(blk):
        return [slice(t * D_MODEL + blk * LANES, t * D_MODEL + (blk + 1) * LANES)
                for t in range(SSM_CHUNK)]

    def pair_rows(blk, q):
        pair = blk * PAIRS_PER_BLOCK + q
        return pair // PAIRS_PER_SG, pl.ds(pair % PAIRS_PER_SG, nc, stride=STATE_PITCH)

    for blk in range(N_LANE_BLOCKS):
        lanes = chunk_lanes(blk)
        u_blk = jnp.concatenate([u_ref[0, :, l] for l in lanes], axis=1)
        wbu = _dot(u_blk, wb_ref[blk])
        for q in range(PAIRS_PER_BLOCK):
            sg, rows = pair_rows(blk, q)
            c0 = 2 * LANES * q
            sre_ref[sg, rows, :] = wbu[:, c0:c0 + LANES]
            sim_ref[sg, rows, :] = wbu[:, c0 + LANES:c0 + 2 * LANES]
        wtu = _dot(u_blk, wt_ref[blk])
        for t, l in enumerate(lanes):
            yt_ref[:, l] = wtu[:, t * LANES:(t + 1) * LANES] + d_ref[:, l] * u_ref[0, :, l].astype(F32)

    ar = [ar_ref[sg] for sg in range(N_SG)]
    ai = [ai_ref[sg] for sg in range(N_SG)]

    def step(c, carry):
        rows = pl.ds(c * STATE_PITCH, PAIRS_PER_SG, stride=1)
        new = []
        for sg in range(N_SG):
            sr, si = carry[2 * sg], carry[2 * sg + 1]
            nr = ar[sg] * sr - ai[sg] * si + sre_ref[sg, rows, :]
            ni = ar[sg] * si + ai[sg] * sr + sim_ref[sg, rows, :]
            sre_ref[sg, rows, :] = sr
            sim_ref[sg, rows, :] = si
            new += [nr, ni]
        return tuple(new)

    init = tuple(carry_ref[i] for i in range(2 * N_SG))
    final = lax.fori_loop(0, nc, step, init, unroll=8)
    for i in range(2 * N_SG):
        carry_ref[i] = final[i]

    for blk in range(N_LANE_BLOCKS):
        parts = []
        for q in range(PAIRS_PER_BLOCK):
            sg, rows = pair_rows(blk, q)
            parts += [sre_ref[sg, rows, :].astype(BF16), sim_ref[sg, rows, :].astype(BF16)]
        wcs = _dot(jnp.concatenate(parts, axis=1), wc_ref[blk])
        for t, l in enumerate(chunk_lanes(blk)):
            y_ref[0, :, l] = (yt_ref[:, l] + wcs[:, t * LANES:(t + 1) * LANES]).astype(y_ref.dtype)


def _ssm_chunk(u3d, wb, wt, wc, ar, ai, d, tokens_per_step):
    bsz, seq, _ = u3d.shape
    nc = tokens_per_step // SSM_CHUNK
    width = SSM_CHUNK * D_MODEL
    u_chunks = u3d.reshape(bsz, seq // SSM_CHUNK, width)
    consts = (wb, wt, wc, ar, ai, d)
    blk = pl.BlockSpec((1, nc, width), lambda b, i: (b, i, 0))
    state = pltpu.VMEM((N_SG, nc * STATE_PITCH, LANES), F32)
    y = pl.pallas_call(
        functools.partial(_ssm_chunk_kernel, nc=nc),
        grid=(bsz, seq // tokens_per_step),
        in_specs=[blk] + [_const_spec(c.shape) for c in consts],
        out_specs=blk,
        out_shape=jax.ShapeDtypeStruct(u_chunks.shape, BF16),
        scratch_shapes=[state, state, pltpu.VMEM((nc, width), F32),
                        pltpu.VMEM((2 * N_SG, SUBLANES, LANES), F32)],
        compiler_params=pltpu.CompilerParams(
            dimension_semantics=("arbitrary", "arbitrary"),
            vmem_limit_bytes=VMEM_LIMIT_BYTES),
        name="ssm_chunk",
    )(u_chunks, *consts)
    return y.reshape(bsz, seq, D_MODEL)


def _mix_ffn_kernel(x1_ref, c_ref, gl_ref, y_ref, wglu_ref, wpw_ref, wout_ref, g2_ref,
                    wup_ref, wdn_ref, gf_ref, o_ref, *, final_norm):
    glu = _dot(jax.nn.gelu(y_ref[...].astype(F32)).astype(BF16), wglu_ref[...])
    ya = glu[:, :D_MODEL] * _sigmoid(glu[:, D_MODEL:])
    yb = _dot(c_ref[...], wpw_ref[...])
    gates = _sigmoid(gl_ref[...].astype(F32))
    mix = gates[:, :D_MODEL] * ya + gates[:, D_MODEL:] * yb
    x2 = x1_ref[...] + _dot(mix.astype(BF16), wout_ref[...])
    h = _rms_norm(x2, g2_ref[...]).astype(BF16)
    x3 = x2 + FFN_SCALE * _swiglu(h, wup_ref, wdn_ref)
    o_ref[...] = _rms_norm(x3, gf_ref[...]) if final_norm else x3


def _mix_ffn(x1, c, gl, y, wglu, wpw, wout, g2, wup, wdn, gf, tm, final_norm):
    n_tok = x1.shape[0]
    row = lambda w: pl.BlockSpec((tm, w), lambda i: (i, 0))
    consts = (wglu, wpw, wout, g2, wup, wdn, gf)
    return pl.pallas_call(
        functools.partial(_mix_ffn_kernel, final_norm=final_norm),
        grid=(n_tok // tm,),
        in_specs=[row(D_MODEL), row(D_MODEL), row(2 * D_MODEL), row(D_MODEL)]
                 + [_const_spec(c.shape) for c in consts],
        out_specs=row(D_MODEL),
        out_shape=jax.ShapeDtypeStruct((n_tok, D_MODEL), F32),
        compiler_params=pltpu.CompilerParams(
            dimension_semantics=("arbitrary",), vmem_limit_bytes=VMEM_LIMIT_BYTES),
        name="mix_ffn",
    )(x1, c, gl, y, *consts)


def _ssm_params(a_re, a_im, log_dt, b_re, b_im, c_re, c_im):
    hi = lax.Precision.HIGHEST
    m = SSM_CHUNK
    dt = jnp.exp(log_dt)[:, None]
    mag = jnp.exp(dt * a_re)
    abar_re = mag * jnp.cos(dt * a_im)
    abar_im = mag * jnp.sin(dt * a_im)
    den = a_re * a_re + a_im * a_im
    nr = abar_re - 1.0
    ni = abar_im
    f_re = (nr * a_re + ni * a_im) / den
    f_im = (ni * a_re - nr * a_im) / den
    bb_re = f_re[..., None] * b_re - f_im[..., None] * b_im
    bb_im = f_re[..., None] * b_im + f_im[..., None] * b_re

    pow_re, pow_im = [jnp.ones_like(abar_re)], [jnp.zeros_like(abar_re)]
    for _ in range(m):
        pr, pi = pow_re[-1], pow_im[-1]
        pow_re.append(pr * abar_re - pi * abar_im)
        pow_im.append(pr * abar_im + pi * abar_re)

    def times_pow(k, x_re, x_im, axis):
        shape = [N_GROUPS, 1, 1]
        shape[axis] = SSM_STATE
        pr, pi = pow_re[k].reshape(shape), pow_im[k].reshape(shape)
        return x_re * pr - x_im * pi, x_re * pi + x_im * pr

    ct_re, ct_im = jnp.swapaxes(c_re, 1, 2), jnp.swapaxes(c_im, 1, 2)

    pair_block = (jnp.arange(SSM_CHUNK * LANES)[:, None] % LANES // (2 * SSM_GROUP)
                  == jnp.arange(BLOCK_STATE)[None, :] // (2 * LANES)).astype(F32)

    def state_operand(re_parts, im_parts):
        a = jnp.stack([jnp.stack([r, i], axis=0) for r, i in zip(re_parts, im_parts)], axis=0)
        a = a.transpose(0, 2, 4, 1, 3)
        a = a.reshape(m, N_GROUPS // 2, 2, SSM_GROUP, 2, SSM_STATE)
        zeros = jnp.zeros_like(a[:, :, 0:1])
        a = jnp.concatenate([jnp.concatenate([a[:, :, 0:1], zeros], axis=-1),
                             jnp.concatenate([zeros, a[:, :, 1:2]], axis=-1)], axis=2)
        a = a.reshape(m, N_LANE_BLOCKS, LANES, 2 * LANES)
        a = a.transpose(1, 0, 2, 3).reshape(N_LANE_BLOCKS, m * LANES, 2 * LANES)
        tiled = jnp.broadcast_to(a[:, :, None, :],
                                 (N_LANE_BLOCKS, m * LANES, PAIRS_PER_BLOCK, 2 * LANES))
        return (tiled.reshape(N_LANE_BLOCKS, m * LANES, BLOCK_STATE) * pair_block).astype(BF16)

    b_parts = [times_pow(m - 1 - t, bb_re, bb_im, 1) for t in range(m)]
    wb = state_operand([p[0] for p in b_parts], [p[1] for p in b_parts])
    c_parts = [times_pow(t + 1, ct_re, ct_im, 1) for t in range(m)]
    wc = jnp.swapaxes(state_operand([p[0] for p in c_parts], [-p[1] for p in c_parts]), 1, 2)

    kern = []
    for k in range(m):
        n_re, n_im = times_pow(k, ct_re, ct_im, 1)
        kern.append(jnp.einsum('gpo,gpi->gio', n_re, bb_re, precision=hi)
                    - jnp.einsum('gpo,gpi->gio', n_im, bb_im, precision=hi))
    zero = jnp.zeros_like(kern[0])
    kt = jnp.stack([jnp.stack([kern[t - t0] if t0 <= t else zero for t in range(m)], axis=0)
                    for t0 in range(m)], axis=0)
    groups_per_block = LANES // SSM_GROUP
    kt = kt.reshape(m, m, N_LANE_BLOCKS, groups_per_block, SSM_GROUP, SSM_GROUP)
    kt = kt.transpose(2, 0, 3, 4, 1, 5)
    wt = (kt[:, :, :, :, :, None, :]
          * jnp.eye(groups_per_block, dtype=F32)[None, None, :, None, None, :, None])
    wt = wt.reshape(N_LANE_BLOCKS, m * LANES, m * LANES).astype(BF16)

    ar = pow_re[m].reshape(N_SG, PAIRS_PER_SG, LANES)
    ai = pow_im[m].reshape(N_SG, PAIRS_PER_SG, LANES)
    return wb, wt, wc, ar, ai


def kernel(x, norm_ffn1_g, w_ffn1_up, w_ffn1_down, norm_mix_g, w_in, ssm_a_re, ssm_a_im, ssm_log_dt, ssm_b_re, ssm_b_im, ssm_c_re, ssm_c_im, ssm_d, w_ssm_glu, conv_dw_w, conv_dw_b, conv_ln_g, conv_ln_b, w_conv_pw, w_out, norm_ffn2_g, w_ffn2_up, w_ffn2_down, norm_final_g):
    bsz, seq, d = x.shape
    depth = w_in.shape[0]
    assert d == D_MODEL and w_ffn1_up.shape[-1] == 2 * D_FF
    tm = 256
    ts = 1024
    tm_mix = 512
    assert seq % tm == 0 and seq % ts == 0 and (bsz * seq) % tm_mix == 0
    row = lambda a: a.reshape(1, -1).astype(F32)

    x2d = x.reshape(bsz * seq, d)
    for l in range(depth):
        conv_w = jnp.pad(conv_dw_w[l].reshape(CONV_WIDTH, d), ((0, 1), (0, 0)))
        x1, u, gl, c = _ffn_conv(
            x2d, row(norm_ffn1_g[l]), w_ffn1_up[l].astype(BF16), w_ffn1_down[l].astype(BF16),
            row(norm_mix_g[l]), w_in[l].astype(BF16), conv_w, row(conv_dw_b[l]),
            row(conv_ln_g[l]), row(conv_ln_b[l]), tm, seq // tm)
        wb, wt, wc, ar, ai = _ssm_params(ssm_a_re[l], ssm_a_im[l], ssm_log_dt[l], ssm_b_re[l],
                                         ssm_b_im[l], ssm_c_re[l], ssm_c_im[l])
        d_chunk = jnp.tile(row(ssm_d[l]), (1, SSM_CHUNK))
        y = _ssm_chunk(u.reshape(bsz, seq, d), wb, wt, wc, ar, ai, d_chunk, ts)
        x2d = _mix_ffn(x1, c, gl, y.reshape(bsz * seq, d), w_ssm_glu[l].astype(BF16),
                       w_conv_pw[l].astype(BF16), w_out[l].astype(BF16), row(norm_ffn2_g[l]),
                       w_ffn2_up[l].astype(BF16), w_ffn2_down[l].astype(BF16),
                       row(norm_final_g), tm_mix, final_norm=(l == depth - 1))
    return x2d.reshape(bsz, seq, d)
```

```python
import functools

import jax
import jax.numpy as jnp
from jax import lax
from jax.experimental import pallas as pl
from jax.experimental.pallas import tpu as pltpu

D_MODEL = 1024
D_FF = 2816
SSM_GROUP = 16
N_GROUPS = 64
SSM_STATE = 64
CONV_WIDTH = 31
RMS_EPS = 1e-6
LN_EPS = 1e-5
FFN_SCALE = 0.5

LANES = 128
SUBLANES = 8
MXU_DIM = 256
VMEM_LIMIT_BYTES = 56 * 1024 * 1024

PAIRS_PER_SG = SUBLANES
N_SG = N_GROUPS // (2 * PAIRS_PER_SG)
SSM_CHUNK = 4
N_LANE_BLOCKS = D_MODEL // LANES
PAIRS_PER_BLOCK = LANES // (2 * SSM_GROUP)
BLOCK_STATE = 2 * LANES * PAIRS_PER_BLOCK
STATE_PITCH = 12
N_SLABS = D_MODEL // LANES
CONV_HALO = 32
CONV_CHAINS = 2

FFN_CHUNKS = ((0, 1536), (1536, D_FF))

BF16 = jnp.bfloat16
F32 = jnp.float32


def _dot(a, b):
    return jnp.dot(a, b, preferred_element_type=F32)


def _rms_norm(x, g):
    return x * lax.rsqrt(jnp.mean(x * x, axis=-1, keepdims=True) + RMS_EPS) * g


def _sigmoid(x):
    return 1.0 / (1.0 + jnp.exp(-x))


def _zero_of(x):
    bits = lax.bitcast_convert_type(x, jnp.uint32)
    return lax.bitcast_convert_type((bits >> 16) >> 16, F32)


def _swiglu(h, wup_ref, wdn_ref):
    acc = None
    for c0, c1 in FFN_CHUNKS:
        gate = _dot(h, wup_ref[:, c0:c1])
        up = _dot(h, wup_ref[:, D_FF + c0:D_FF + c1])
        act = (gate * _sigmoid(gate) * up).astype(BF16)
        part = _dot(act, wdn_ref[c0:c1, :])
        acc = part if acc is None else acc + part
    return acc


def _const_spec(shape):
    nd = len(shape)
    return pl.BlockSpec(shape, lambda *_: (0,) * nd, pipeline_mode=pl.Buffered(1))


def _ffn_conv_kernel(x_ref, g1_ref, wup_ref, wdn_ref, gm_ref, win_ref,
                     cw_ref, cb_ref, lng_ref, lnb_ref,
                     x1_ref, u_ref, gl_ref, c_ref, z_ref, cv_ref, *, tm, blocks_per_seq):
    g = pl.program_id(0)

    @pl.when(g == 0)
    def _():
        z_ref[...] = jnp.zeros_like(z_ref)

    base = CONV_HALO - (CONV_WIDTH - 1)
    tiles = [(c, r) for c in range(N_SLABS) for r in range(tm // SUBLANES)]
    tiles_per_chain = len(tiles) // CONV_CHAINS
    prev = [None] * CONV_CHAINS
    for i in range(tiles_per_chain):
        for ch in range(CONV_CHAINS):
            c, r = tiles[ch * tiles_per_chain + i]
            lanes = slice(c * LANES, (c + 1) * LANES)
            acc = jnp.broadcast_to(cb_ref[:, lanes], (SUBLANES, LANES))
            if prev[ch] is not None:
                acc = acc + _zero_of(prev[ch])
            for k in range(CONV_WIDTH):
                rows = pl.ds(base + k + r * SUBLANES, SUBLANES, stride=1)
                acc = acc + cw_ref[k:k + 1, lanes] * z_ref[c, rows, :]
            cv_ref[r * SUBLANES:(r + 1) * SUBLANES, lanes] = acc
            prev[ch] = acc

    cv = cv_ref[...]
    mu = jnp.mean(cv, axis=-1, keepdims=True)
    xc = cv - mu
    var = jnp.mean(xc * xc, axis=-1, keepdims=True)
    ln = xc * lax.rsqrt(var + LN_EPS) * lng_ref[...] + lnb_ref[...]
    c_ref[...] = (ln * _sigmoid(ln)).astype(c_ref.dtype)

    x = x_ref[...]
    h = _rms_norm(x, g1_ref[...]).astype(BF16)
    x1 = x + FFN_SCALE * _swiglu(h, wup_ref, wdn_ref)
    x1_ref[...] = x1
    h2 = _rms_norm(x1, gm_ref[...]).astype(BF16)
    u_ref[...] = _dot(h2, win_ref[:, 0:D_MODEL]).astype(u_ref.dtype)
    gl_ref[...] = _dot(h2, win_ref[:, 3 * D_MODEL:5 * D_MODEL]).astype(gl_ref.dtype)
    v = _dot(h2, win_ref[:, D_MODEL:3 * D_MODEL])
    z = v[:, :D_MODEL] * _sigmoid(v[:, D_MODEL:])

    seq_start = (g % blocks_per_seq) == 0
    for c in range(N_SLABS):
        tail = z_ref[c, tm:tm + CONV_HALO, :]
        z_ref[c, 0:CONV_HALO, :] = jnp.where(seq_start, 0.0, tail)
        z_ref[c, CONV_HALO:CONV_HALO + tm, :] = z[:, c * LANES:(c + 1) * LANES]


def _ffn_conv(x2d, g1, wup, wdn, gm, win, cw, cb, lng, lnb, tm, blocks_per_seq):
    n_tok = x2d.shape[0]
    n_blk = n_tok // tm
    cur = lambda w: pl.BlockSpec((tm, w), lambda i: (jnp.minimum(i, n_blk - 1), 0))
    prev = lambda w: pl.BlockSpec((tm, w), lambda i: (jnp.maximum(i - 1, 0), 0))
    consts = (g1, wup, wdn, gm, win, cw, cb, lng, lnb)
    return pl.pallas_call(
        functools.partial(_ffn_conv_kernel, tm=tm, blocks_per_seq=blocks_per_seq),
        grid=(n_blk + 1,),
        in_specs=[cur(D_MODEL)] + [_const_spec(c.shape) for c in consts],
        out_specs=[cur(D_MODEL), cur(D_MODEL), cur(2 * D_MODEL), prev(D_MODEL)],
        out_shape=[jax.ShapeDtypeStruct((n_tok, D_MODEL), F32),
                   jax.ShapeDtypeStruct((n_tok, D_MODEL), BF16),
                   jax.ShapeDtypeStruct((n_tok, 2 * D_MODEL), BF16),
                   jax.ShapeDtypeStruct((n_tok, D_MODEL), BF16)],
        scratch_shapes=[pltpu.VMEM((N_SLABS, CONV_HALO + tm, LANES), F32),
                        pltpu.VMEM((tm, D_MODEL), F32)],
        compiler_params=pltpu.CompilerParams(
            dimension_semantics=("arbitrary",), vmem_limit_bytes=VMEM_LIMIT_BYTES),
        name="ffn_conv",
    )(x2d, *consts)


def _ssm_chunk_kernel(u_ref, wb_ref, wt_ref, wc_ref, ar_ref, ai_ref, d_ref,
                      y_ref, sre_ref, sim_ref, yt_ref, carry_ref, *, nc):
    @pl.when(pl.program_id(1) == 0)
    def _():
        carry_ref[...] = jnp.zeros_like(carry_ref)

    def chunk_lanes(blk):
        return [slice(t * D_MODEL + blk * LANES, t * D_MODEL + (blk + 1) * LANES)
                for t in range(SSM_CHUNK)]

    def pair_rows(blk, q):
        pair = blk * PAIRS_PER_BLOCK + q
        return pair // PAIRS_PER_SG, pl.ds(pair % PAIRS_PER_SG, nc, stride=STATE_PITCH)

    for blk in range(N_LANE_BLOCKS):
        lanes = chunk_lanes(blk)
        u_blk = jnp.concatenate([u_ref[0, :, l] for l in lanes], axis=1)
        wbu = _dot(u_blk, wb_ref[blk])
        for q in range(PAIRS_PER_BLOCK):
            sg, rows = pair_rows(blk, q)
            c0 = 2 * LANES * q
            sre_ref[sg, rows, :] = wbu[:, c0:c0 + LANES]
            sim_ref[sg, rows, :] = wbu[:, c0 + LANES:c0 + 2 * LANES]
        wtu = _dot(u_blk, wt_ref[blk])
        for t, l in enumerate(lanes):
            yt_ref[:, l] = wtu[:, t * LANES:(t + 1) * LANES] + d_ref[:, l] * u_ref[0, :, l].astype(F32)

    ar = [ar_ref[sg] for sg in range(N_SG)]
    ai = [ai_ref[sg] for sg in range(N_SG)]

    def step(c, carry):
        rows = pl.ds(c * STATE_PITCH, PAIRS_PER_SG, stride=1)
        new = []
        for sg in range(N_SG):
            sr, si = carry[2 * sg], carry[2 * sg + 1]
            nr = ar[sg] * sr - ai[sg] * si + sre_ref[sg, rows, :]
            ni = ar[sg] * si + ai[sg] * sr + sim_ref[sg, rows, :]
            sre_ref[sg, rows, :] = sr
            sim_ref[sg, rows, :] = si
            new += [nr, ni]
        return tuple(new)

    init = tuple(carry_ref[i] for i in range(2 * N_SG))
    final = lax.fori_loop(0, nc, step, init, unroll=8)
    for i in range(2 * N_SG):
        carry_ref[i] = final[i]

    for blk in range(N_LANE_BLOCKS):
        parts = []
        for q in range(PAIRS_PER_BLOCK):
            sg, rows = pair_rows(blk, q)
            parts += [sre_ref[sg, rows, :].astype(BF16), sim_ref[sg, rows, :].astype(BF16)]
        wcs = _dot(jnp.concatenate(parts, axis=1), wc_ref[blk])
        for t, l in enumerate(chunk_lanes(blk)):
            y_ref[0, :, l] = (yt_ref[:, l] + wcs[:, t * LANES:(t + 1) * LANES]).astype(y_ref.dtype)


def _ssm_chunk(u3d, wb, wt, wc, ar, ai, d, tokens_per_step):
    bsz, seq, _ = u3d.shape
    nc = tokens_per_step // SSM_CHUNK
    width = SSM_CHUNK * D_MODEL
    u_chunks = u3d.reshape(bsz, seq // SSM_CHUNK, width)
    consts = (wb, wt, wc, ar, ai, d)
    blk = pl.BlockSpec((1, nc, width), lambda b, i: (b, i, 0))
    state = pltpu.VMEM((N_SG, nc * STATE_PITCH, LANES), F32)
    y = pl.pallas_call(
        functools.partial(_ssm_chunk_kernel, nc=nc),
        grid=(bsz, seq // tokens_per_step),
        in_specs=[blk] + [_const_spec(c.shape) for c in consts],
        out_specs=blk,
        out_shape=jax.ShapeDtypeStruct(u_chunks.shape, BF16),
        scratch_shapes=[state, state, pltpu.VMEM((nc, width), F32),
                        pltpu.VMEM((2 * N_SG, SUBLANES, LANES), F32)],
        compiler_params=pltpu.CompilerParams(
            dimension_semantics=("arbitrary", "arbitrary"),
            vmem_limit_bytes=VMEM_LIMIT_BYTES),
        name="ssm_chunk",
    )(u_chunks, *consts)
    return y.reshape(bsz, seq, D_MODEL)


def _mix_ffn_kernel(x1_ref, c_ref, gl_ref, y_ref, wglu_ref, wpw_ref, wout_ref, g2_ref,
                    wup_ref, wdn_ref, gf_ref, o_ref, *, final_norm):
    glu = _dot(jax.nn.gelu(y_ref[...].astype(F32)).astype(BF16), wglu_ref[...])
    ya = glu[:, :D_MODEL] * _sigmoid(glu[:, D_MODEL:])
    yb = _dot(c_ref[...], wpw_ref[...])
    gates = _sigmoid(gl_ref[...].astype(F32))
    mix = gates[:, :D_MODEL] * ya + gates[:, D_MODEL:] * yb
    x2 = x1_ref[...] + _dot(mix.astype(BF16), wout_ref[...])
    h = _rms_norm(x2, g2_ref[...]).astype(BF16)
    x3 = x2 + FFN_SCALE * _swiglu(h, wup_ref, wdn_ref)
    o_ref[...] = _rms_norm(x3, gf_ref[...]) if final_norm else x3


def _mix_ffn(x1, c, gl, y, wglu, wpw, wout, g2, wup, wdn, gf, tm, final_norm):
    n_tok = x1.shape[0]
    row = lambda w: pl.BlockSpec((tm, w), lambda i: (i, 0))
    consts = (wglu, wpw, wout, g2, wup, wdn, gf)
    return pl.pallas_call(
        functools.partial(_mix_ffn_kernel, final_norm=final_norm),
        grid=(n_tok // tm,),
        in_specs=[row(D_MODEL), row(D_MODEL), row(2 * D_MODEL), row(D_MODEL)]
                 + [_const_spec(c.shape) for c in consts],
        out_specs=row(D_MODEL),
        out_shape=jax.ShapeDtypeStruct((n_tok, D_MODEL), F32),
        compiler_params=pltpu.CompilerParams(
            dimension_semantics=("arbitrary",), vmem_limit_bytes=VMEM_LIMIT_BYTES),
        name="mix_ffn",
    )(x1, c, gl, y, *consts)


def _ssm_params(a_re, a_im, log_dt, b_re, b_im, c_re, c_im):
    hi = lax.Precision.HIGHEST
    m = SSM_CHUNK
    dt = jnp.exp(log_dt)[:, None]
    mag = jnp.exp(dt * a_re)
    abar_re = mag * jnp.cos(dt * a_im)
    abar_im = mag * jnp.sin(dt * a_im)
    den = a_re * a_re + a_im * a_im
    nr = abar_re - 1.0
    ni = abar_im
    f_re = (nr * a_re + ni * a_im) / den
    f_im = (ni * a_re - nr * a_im) / den
    bb_re = f_re[..., None] * b_re - f_im[..., None] * b_im
    bb_im = f_re[..., None] * b_im + f_im[..., None] * b_re

    pow_re, pow_im = [jnp.ones_like(abar_re)], [jnp.zeros_like(abar_re)]
    for _ in range(m):
        pr, pi = pow_re[-1], pow_im[-1]
        pow_re.append(pr * abar_re - pi * abar_im)
        pow_im.append(pr * abar_im + pi * abar_re)

    def times_pow(k, x_re, x_im, axis):
        shape = [N_GROUPS, 1, 1]
        shape[axis] = SSM_STATE
        pr, pi = pow_re[k].reshape(shape), pow_im[k].reshape(shape)
        return x_re * pr - x_im * pi, x_re * pi + x_im * pr

    ct_re, ct_im = jnp.swapaxes(c_re, 1, 2), jnp.swapaxes(c_im, 1, 2)

    groups_per_block = LANES // SSM_GROUP
    lane = jnp.arange(LANES)[None, :]

    def lane_tile(x):
        return jnp.concatenate([x] * (LANES // x.shape[-1]), axis=-1)

    def by_block(parts):
        x = jnp.stack(parts, axis=0)
        n, _, a, b = x.shape
        x = x.reshape(n, N_LANE_BLOCKS, groups_per_block, a, b).transpose(1, 0, 2, 3, 4)
        return x.reshape(N_LANE_BLOCKS, n * groups_per_block * a, b)

    row = jnp.arange(m * LANES)[:, None] % LANES
    row_pair, row_e = row // (2 * SSM_GROUP), row // SSM_GROUP % 2
    in_group_lanes = (lane // SSM_STATE == row_e).astype(F32)
    b_parts = [times_pow(m - 1 - t, bb_re, bb_im, 1) for t in range(m)]
    b_parts = [[jnp.swapaxes(p[c], 1, 2) for p in b_parts] for c in range(2)]
    wb = jnp.concatenate(
        [lane_tile(by_block(b_parts[c])) * (in_group_lanes * (row_pair == q))
         for q in range(PAIRS_PER_BLOCK) for c in range(2)], axis=-1).astype(BF16)

    row = jnp.arange(BLOCK_STATE)[:, None]
    row_group = row // (2 * LANES) * 2 + row % LANES // SSM_STATE
    own_group_lanes = (lane // SSM_GROUP == row_group).astype(F32)
    wc_pieces = []
    for t in range(m):
        n_re, n_im = times_pow(t + 1, ct_re, ct_im, 1)
        x = jnp.stack([n_re, -n_im], axis=0)
        x = x.reshape(2, N_LANE_BLOCKS, PAIRS_PER_BLOCK, LANES, SSM_GROUP)
        x = x.transpose(1, 2, 0, 3, 4).reshape(N_LANE_BLOCKS, BLOCK_STATE, SSM_GROUP)
        wc_pieces.append(lane_tile(x) * own_group_lanes)
    wc = jnp.concatenate(wc_pieces, axis=-1).astype(BF16)

    kern = []
    for k in range(m):
        n_re, n_im = times_pow(k, ct_re, ct_im, 1)
        kern.append(jnp.einsum('gpo,gpi->gio', n_re, bb_re, precision=hi)
                    - jnp.einsum('gpo,gpi->gio', n_im, bb_im, precision=hi))
    zero = jnp.zeros_like(kern[0])
    row_group = jnp.arange(m * LANES)[:, None] % LANES // SSM_GROUP
    own_group_lanes = (lane // SSM_GROUP == row_group).astype(F32)
    wt = jnp.concatenate(
        [lane_tile(by_block([kern[t - t0] if t0 <= t else zero for t0 in range(m)]))
         * own_group_lanes for t in range(m)], axis=-1).astype(BF16)

    ar = pow_re[m].reshape(N_SG, PAIRS_PER_SG, LANES)
    ai = pow_im[m].reshape(N_SG, PAIRS_PER_SG, LANES)
    return wb, wt, wc, ar, ai


def kernel(x, norm_ffn1_g, w_ffn1_up, w_ffn1_down, norm_mix_g, w_in, ssm_a_re, ssm_a_im, ssm_log_dt, ssm_b_re, ssm_b_im, ssm_c_re, ssm_c_im, ssm_d, w_ssm_glu, conv_dw_w, conv_dw_b, conv_ln_g, conv_ln_b, w_conv_pw, w_out, norm_ffn2_g, w_ffn2_up, w_ffn2_down, norm_final_g):
    bsz, seq, d = x.shape
    depth = w_in.shape[0]
    assert d == D_MODEL and w_ffn1_up.shape[-1] == 2 * D_FF
    tm = 256
    ts = 1024
    tm_mix = 512
    assert seq % tm == 0 and seq % ts == 0 and (bsz * seq) % tm_mix == 0
    row = lambda a: a.reshape(1, -1).astype(F32)

    x2d = x.reshape(bsz * seq, d)
    for l in range(depth):
        conv_w = jnp.pad(conv_dw_w[l].reshape(CONV_WIDTH, d), ((0, 1), (0, 0)))
        x1, u, gl, c = _ffn_conv(
            x2d, row(norm_ffn1_g[l]), w_ffn1_up[l].astype(BF16), w_ffn1_down[l].astype(BF16),
            row(norm_mix_g[l]), w_in[l].astype(BF16), conv_w, row(conv_dw_b[l]),
            row(conv_ln_g[l]), row(conv_ln_b[l]), tm, seq // tm)
        wb, wt, wc, ar, ai = _ssm_params(ssm_a_re[l], ssm_a_im[l], ssm_log_dt[l], ssm_b_re[l],
                                         ssm_b_im[l], ssm_c_re[l], ssm_c_im[l])
        d_chunk = jnp.tile(row(ssm_d[l]), (1, SSM_CHUNK))
        y = _ssm_chunk(u.reshape(bsz, seq, d), wb, wt, wc, ar, ai, d_chunk, ts)
        x2d = _mix_ffn(x1, c, gl, y.reshape(bsz * seq, d), w_ssm_glu[l].astype(BF16),
                       w_conv_pw[l].astype(BF16), w_out[l].astype(BF16), row(norm_ffn2_g[l]),
                       w_ffn2_up[l].astype(BF16), w_ffn2_down[l].astype(BF16),
                       row(norm_final_g), tm_mix, final_norm=(l == depth - 1))
    return x2d.reshape(bsz, seq, d)
```

```python
import functools

import jax
import jax.numpy as jnp
from jax import lax
from jax.experimental import pallas as pl
from jax.experimental.pallas import tpu as pltpu

D_MODEL = 1024
D_FF = 2816
SSM_GROUP = 16
N_GROUPS = 64
SSM_STATE = 64
CONV_WIDTH = 31
RMS_EPS = 1e-6
LN_EPS = 1e-5
FFN_SCALE = 0.5

LANES = 128
SUBLANES = 8
MXU_DIM = 256
VMEM_LIMIT_BYTES = 56 * 1024 * 1024

PAIRS_PER_SG = SUBLANES
N_SG = N_GROUPS // (2 * PAIRS_PER_SG)
SSM_CHUNK = 4
N_LANE_BLOCKS = D_MODEL // LANES
PAIRS_PER_BLOCK = LANES // (2 * SSM_GROUP)
BLOCK_STATE = 2 * LANES * PAIRS_PER_BLOCK
STATE_PITCH = 12
N_SLABS = D_MODEL // LANES
CONV_HALO = 32
CONV_CHAINS = 2

FFN_CHUNKS = ((0, 1536), (1536, D_FF))

BF16 = jnp.bfloat16
F32 = jnp.float32


def _dot(a, b):
    return jnp.dot(a, b, preferred_element_type=F32)


def _rms_norm(x, g):
    return x * lax.rsqrt(jnp.mean(x * x, axis=-1, keepdims=True) + RMS_EPS) * g


def _sigmoid(x):
    return 1.0 / (1.0 + jnp.exp(-x))


def _zero_of(x):
    bits = lax.bitcast_convert_type(x, jnp.uint32)
    return lax.bitcast_convert_type((bits >> 16) >> 16, F32)


def _swiglu(h, wup_ref, wdn_ref):
    acc = None
    for c0, c1 in FFN_CHUNKS:
        gate = _dot(h, wup_ref[:, c0:c1])
        up = _dot(h, wup_ref[:, D_FF + c0:D_FF + c1])
        act = (gate * _sigmoid(gate) * up).astype(BF16)
        part = _dot(act, wdn_ref[c0:c1, :])
        acc = part if acc is None else acc + part
    return acc


def _const_spec(shape):
    nd = len(shape)
    return pl.BlockSpec(shape, lambda *_: (0,) * nd, pipeline_mode=pl.Buffered(1))


def _ffn_conv_kernel(x_ref, g1_ref, wup_ref, wdn_ref, gm_ref, win_ref,
                     cw_ref, cb_ref, lng_ref, lnb_ref,
                     x1_ref, u_ref, gl_ref, c_ref, z_ref, cv_ref, us_ref, *, tm, blocks_per_seq):
    g = pl.program_id(0)

    @pl.when(g == 0)
    def _():
        z_ref[...] = jnp.zeros_like(z_ref)

    base = CONV_HALO - (CONV_WIDTH - 1)
    tiles = [(c, r) for c in range(N_SLABS) for r in range(tm // SUBLANES)]
    tiles_per_chain = len(tiles) // CONV_CHAINS
    prev = [None] * CONV_CHAINS
    for i in range(tiles_per_chain):
        for ch in range(CONV_CHAINS):
            c, r = tiles[ch * tiles_per_chain + i]
            lanes = slice(c * LANES, (c + 1) * LANES)
            acc = jnp.broadcast_to(cb_ref[:, lanes], (SUBLANES, LANES))
            if prev[ch] is not None:
                acc = acc + _zero_of(prev[ch])
            for k in range(CONV_WIDTH):
                rows = pl.ds(base + k + r * SUBLANES, SUBLANES, stride=1)
                acc = acc + cw_ref[k:k + 1, lanes] * z_ref[c, rows, :]
            cv_ref[r * SUBLANES:(r + 1) * SUBLANES, lanes] = acc
            prev[ch] = acc

    cv = cv_ref[...]
    mu = jnp.mean(cv, axis=-1, keepdims=True)
    xc = cv - mu
    var = jnp.mean(xc * xc, axis=-1, keepdims=True)
    ln = xc * lax.rsqrt(var + LN_EPS) * lng_ref[...] + lnb_ref[...]
    c_ref[...] = (ln * _sigmoid(ln)).astype(c_ref.dtype)

    x = x_ref[...]
    h = _rms_norm(x, g1_ref[...]).astype(BF16)
    x1 = x + FFN_SCALE * _swiglu(h, wup_ref, wdn_ref)
    x1_ref[...] = x1
    h2 = _rms_norm(x1, gm_ref[...]).astype(BF16)
    u = _dot(h2, win_ref[:, 0:D_MODEL])
    for s in range(N_SLABS):
        us_ref[s] = u[:, s * LANES:(s + 1) * LANES]
    for t in range(SSM_CHUNK):
        for s in range(N_SLABS):
            lanes = slice(t * D_MODEL + s * LANES, t * D_MODEL + (s + 1) * LANES)
            rows = pl.ds(t, tm // SSM_CHUNK, stride=SSM_CHUNK)
            u_ref[:, lanes] = us_ref[s, rows, :].astype(u_ref.dtype)
    gl_ref[...] = _dot(h2, win_ref[:, 3 * D_MODEL:5 * D_MODEL]).astype(gl_ref.dtype)
    v = _dot(h2, win_ref[:, D_MODEL:3 * D_MODEL])
    z = v[:, :D_MODEL] * _sigmoid(v[:, D_MODEL:])

    seq_start = (g % blocks_per_seq) == 0
    for c in range(N_SLABS):
        tail = z_ref[c, tm:tm + CONV_HALO, :]
        z_ref[c, 0:CONV_HALO, :] = jnp.where(seq_start, 0.0, tail)
        z_ref[c, CONV_HALO:CONV_HALO + tm, :] = z[:, c * LANES:(c + 1) * LANES]


def _ffn_conv(x2d, g1, wup, wdn, gm, win, cw, cb, lng, lnb, tm, blocks_per_seq):
    n_tok = x2d.shape[0]
    n_blk = n_tok // tm
    cur = lambda w: pl.BlockSpec((tm, w), lambda i: (jnp.minimum(i, n_blk - 1), 0))
    prev = lambda w: pl.BlockSpec((tm, w), lambda i: (jnp.maximum(i - 1, 0), 0))
    consts = (g1, wup, wdn, gm, win, cw, cb, lng, lnb)
    return pl.pallas_call(
        functools.partial(_ffn_conv_kernel, tm=tm, blocks_per_seq=blocks_per_seq),
        grid=(n_blk + 1,),
        in_specs=[cur(D_MODEL)] + [_const_spec(c.shape) for c in consts],
        out_specs=[cur(D_MODEL),
                   pl.BlockSpec((tm // SSM_CHUNK, SSM_CHUNK * D_MODEL),
                                lambda i: (jnp.minimum(i, n_blk - 1), 0)),
                   cur(2 * D_MODEL), prev(D_MODEL)],
        out_shape=[jax.ShapeDtypeStruct((n_tok, D_MODEL), F32),
                   jax.ShapeDtypeStruct((n_tok // SSM_CHUNK, SSM_CHUNK * D_MODEL), BF16),
                   jax.ShapeDtypeStruct((n_tok, 2 * D_MODEL), BF16),
                   jax.ShapeDtypeStruct((n_tok, D_MODEL), BF16)],
        scratch_shapes=[pltpu.VMEM((N_SLABS, CONV_HALO + tm, LANES), F32),
                        pltpu.VMEM((tm, D_MODEL), F32),
                        pltpu.VMEM((N_SLABS, tm, LANES), F32)],
        compiler_params=pltpu.CompilerParams(
            dimension_semantics=("arbitrary",), vmem_limit_bytes=VMEM_LIMIT_BYTES),
        name="ffn_conv",
    )(x2d, *consts)


def _ssm_chunk_kernel(u_ref, wb_ref, wt_ref, wc_ref, ar_ref, ai_ref, d_ref,
                      y_ref, sre_ref, sim_ref, yt_ref, carry_ref, *, nc):
    @pl.when(pl.program_id(1) == 0)
    def _():
        carry_ref[...] = jnp.zeros_like(carry_ref)

    def chunk_lanes(blk):
        return [slice(t * D_MODEL + blk * LANES, t * D_MODEL + (blk + 1) * LANES)
                for t in range(SSM_CHUNK)]

    def pair_rows(blk, q):
        pair = blk * PAIRS_PER_BLOCK + q
        return pair // PAIRS_PER_SG, pl.ds(pair % PAIRS_PER_SG, nc, stride=STATE_PITCH)

    for blk in range(N_LANE_BLOCKS):
        lanes = chunk_lanes(blk)
        u_blk = jnp.concatenate([u_ref[0, :, l] for l in lanes], axis=1)
        wbu = _dot(u_blk, wb_ref[blk])
        for q in range(PAIRS_PER_BLOCK):
            sg, rows = pair_rows(blk, q)
            c0 = 2 * LANES * q
            sre_ref[sg, rows, :] = wbu[:, c0:c0 + LANES]
            sim_ref[sg, rows, :] = wbu[:, c0 + LANES:c0 + 2 * LANES]
        wtu = _dot(u_blk, wt_ref[blk])
        for t, l in enumerate(lanes):
            yt_ref[:, l] = wtu[:, t * LANES:(t + 1) * LANES] + d_ref[:, l] * u_ref[0, :, l].astype(F32)

    ar = [ar_ref[sg] for sg in range(N_SG)]
    ai = [ai_ref[sg] for sg in range(N_SG)]

    def step(c, carry):
        rows = pl.ds(c * STATE_PITCH, PAIRS_PER_SG, stride=1)
        new = []
        for sg in range(N_SG):
            sr, si = carry[2 * sg], carry[2 * sg + 1]
            nr = ar[sg] * sr - ai[sg] * si + sre_ref[sg, rows, :]
            ni = ar[sg] * si + ai[sg] * sr + sim_ref[sg, rows, :]
            sre_ref[sg, rows, :] = sr
            sim_ref[sg, rows, :] = si
            new += [nr, ni]
        return tuple(new)

    init = tuple(carry_ref[i] for i in range(2 * N_SG))
    final = lax.fori_loop(0, nc, step, init, unroll=8)
    for i in range(2 * N_SG):
        carry_ref[i] = final[i]

    for blk in range(N_LANE_BLOCKS):
        parts = []
        for q in range(PAIRS_PER_BLOCK):
            sg, rows = pair_rows(blk, q)
            parts += [sre_ref[sg, rows, :].astype(BF16), sim_ref[sg, rows, :].astype(BF16)]
        wcs = _dot(jnp.concatenate(parts, axis=1), wc_ref[blk])
        for t, l in enumerate(chunk_lanes(blk)):
            y_ref[0, :, l] = (yt_ref[:, l] + wcs[:, t * LANES:(t + 1) * LANES]).astype(y_ref.dtype)


def _ssm_chunk(u_chunks, wb, wt, wc, ar, ai, d, tokens_per_step):
    bsz, n_chunks, width = u_chunks.shape
    nc = tokens_per_step // SSM_CHUNK
    consts = (wb, wt, wc, ar, ai, d)
    blk = pl.BlockSpec((1, nc, width), lambda b, i: (b, i, 0))
    state = pltpu.VMEM((N_SG, nc * STATE_PITCH, LANES), F32)
    return pl.pallas_call(
        functools.partial(_ssm_chunk_kernel, nc=nc),
        grid=(bsz, n_chunks // nc),
        in_specs=[blk] + [_const_spec(c.shape) for c in consts],
        out_specs=blk,
        out_shape=jax.ShapeDtypeStruct(u_chunks.shape, BF16),
        scratch_shapes=[state, state, pltpu.VMEM((nc, width), F32),
                        pltpu.VMEM((2 * N_SG, SUBLANES, LANES), F32)],
        compiler_params=pltpu.CompilerParams(
            dimension_semantics=("arbitrary", "arbitrary"),
            vmem_limit_bytes=VMEM_LIMIT_BYTES),
        name="ssm_chunk",
    )(u_chunks, *consts)


def _mix_ffn_kernel(x1_ref, c_ref, gl_ref, y_ref, wglu_ref, wpw_ref, wout_ref, g2_ref,
                    wup_ref, wdn_ref, gf_ref, o_ref, ys_ref, *, tm, final_norm):
    for t in range(SSM_CHUNK):
        for s in range(N_SLABS):
            lanes = slice(t * D_MODEL + s * LANES, t * D_MODEL + (s + 1) * LANES)
            rows = pl.ds(t, tm // SSM_CHUNK, stride=SSM_CHUNK)
            ys_ref[s, rows, :] = y_ref[:, lanes].astype(F32)
    y = jnp.concatenate([ys_ref[s] for s in range(N_SLABS)], axis=1)
    glu = _dot(jax.nn.gelu(y).astype(BF16), wglu_ref[...])
    ya = glu[:, :D_MODEL] * _sigmoid(glu[:, D_MODEL:])
    yb = _dot(c_ref[...], wpw_ref[...])
    gates = _sigmoid(gl_ref[...].astype(F32))
    mix = gates[:, :D_MODEL] * ya + gates[:, D_MODEL:] * yb
    x2 = x1_ref[...] + _dot(mix.astype(BF16), wout_ref[...])
    h = _rms_norm(x2, g2_ref[...]).astype(BF16)
    x3 = x2 + FFN_SCALE * _swiglu(h, wup_ref, wdn_ref)
    o_ref[...] = _rms_norm(x3, gf_ref[...]) if final_norm else x3


def _mix_ffn(x1, c, gl, y, wglu, wpw, wout, g2, wup, wdn, gf, tm, final_norm):
    n_tok = x1.shape[0]
    row = lambda w: pl.BlockSpec((tm, w), lambda i: (i, 0))
    consts = (wglu, wpw, wout, g2, wup, wdn, gf)
    y_spec = pl.BlockSpec((tm // SSM_CHUNK, SSM_CHUNK * D_MODEL), lambda i: (i, 0))
    return pl.pallas_call(
        functools.partial(_mix_ffn_kernel, tm=tm, final_norm=final_norm),
        grid=(n_tok // tm,),
        in_specs=[row(D_MODEL), row(D_MODEL), row(2 * D_MODEL), y_spec]
                 + [_const_spec(c.shape) for c in consts],
        out_specs=row(D_MODEL),
        out_shape=jax.ShapeDtypeStruct((n_tok, D_MODEL), F32),
        scratch_shapes=[pltpu.VMEM((N_SLABS, tm, LANES), F32)],
        compiler_params=pltpu.CompilerParams(
            dimension_semantics=("arbitrary",), vmem_limit_bytes=VMEM_LIMIT_BYTES),
        name="mix_ffn",
    )(x1, c, gl, y, *consts)


def _expand_kernel(b_ref, t_ref, c_ref, eb_ref, et_ref, mb_ref, mt_ref, mc_ref,
                   wb_ref, wt_ref, wc_ref):
    wb_ref[0] = (_dot(b_ref[0].astype(BF16), eb_ref[...]) * mb_ref[...]).astype(BF16)
    wt_ref[0] = (_dot(t_ref[0].astype(BF16), et_ref[...]) * mt_ref[...]).astype(BF16)
    wc_ref[0] = (_dot(c_ref[0].astype(BF16), et_ref[...]) * mc_ref[...]).astype(BF16)


def _expand_operands(wb_compact, wt_compact, wc_compact):
    m = SSM_CHUNK
    idx = lambda n: jnp.arange(n)
    src, dst = idx(2 * SSM_STATE)[:, None], idx(BLOCK_STATE)[None, :]
    spread_b = ((src // SSM_STATE == dst // LANES % 2)
                & (src % SSM_STATE == dst % SSM_STATE)).astype(BF16)
    row_group = idx(m * LANES)[:, None] % LANES // SSM_GROUP
    mask_b = ((row_group // 2 == dst // (2 * LANES))
              & (row_group % 2 == dst % LANES // SSM_STATE)).astype(F32)
    src, dst = idx(m * SSM_GROUP)[:, None], idx(m * LANES)[None, :]
    spread_t = ((src // SSM_GROUP == dst // LANES)
                & (src % SSM_GROUP == dst % SSM_GROUP)).astype(BF16)
    mask_t = (row_group == dst % LANES // SSM_GROUP).astype(F32)
    row = idx(BLOCK_STATE)[:, None]
    row_group = row // (2 * LANES) * 2 + row % LANES // SSM_STATE
    mask_c = (row_group == dst % LANES // SSM_GROUP).astype(F32)

    consts = (spread_b, spread_t, mask_b, mask_t, mask_c)
    per_block = lambda a: pl.BlockSpec((1,) + a.shape[1:], lambda i: (i, 0, 0))
    shapes = [(N_LANE_BLOCKS, m * LANES, BLOCK_STATE), (N_LANE_BLOCKS, m * LANES, m * LANES),
              (N_LANE_BLOCKS, BLOCK_STATE, m * LANES)]
    return pl.pallas_call(
        _expand_kernel,
        grid=(N_LANE_BLOCKS,),
        in_specs=[per_block(wb_compact), per_block(wt_compact), per_block(wc_compact)]
                 + [pl.BlockSpec(c.shape, lambda i: (0, 0)) for c in consts],
        out_specs=[pl.BlockSpec((1,) + s[1:], lambda i: (i, 0, 0)) for s in shapes],
        out_shape=[jax.ShapeDtypeStruct(s, BF16) for s in shapes],
        compiler_params=pltpu.CompilerParams(dimension_semantics=("arbitrary",)),
        name="expand_operands",
    )(wb_compact, wt_compact, wc_compact, *consts)


def _ssm_params(a_re, a_im, log_dt, b_re, b_im, c_re, c_im):
    hi = lax.Precision.HIGHEST
    m = SSM_CHUNK
    dt = jnp.exp(log_dt)[:, None]
    mag = jnp.exp(dt * a_re)
    abar_re = mag * jnp.cos(dt * a_im)
    abar_im = mag * jnp.sin(dt * a_im)
    den = a_re * a_re + a_im * a_im
    nr = abar_re - 1.0
    ni = abar_im
    f_re = (nr * a_re + ni * a_im) / den
    f_im = (ni * a_re - nr * a_im) / den
    bb_re = f_re[..., None] * b_re - f_im[..., None] * b_im
    bb_im = f_re[..., None] * b_im + f_im[..., None] * b_re

    pow_re, pow_im = [jnp.ones_like(abar_re)], [jnp.zeros_like(abar_re)]
    for _ in range(m):
        pr, pi = pow_re[-1], pow_im[-1]
        pow_re.append(pr * abar_re - pi * abar_im)
        pow_im.append(pr * abar_im + pi * abar_re)

    def times_pow(k, x_re, x_im, axis):
        shape = [N_GROUPS, 1, 1]
        shape[axis] = SSM_STATE
        pr, pi = pow_re[k].reshape(shape), pow_im[k].reshape(shape)
        return x_re * pr - x_im * pi, x_re * pi + x_im * pr

    ct_re, ct_im = jnp.swapaxes(c_re, 1, 2), jnp.swapaxes(c_im, 1, 2)

    groups_per_block = LANES // SSM_GROUP
    nb = N_LANE_BLOCKS

    b_parts = [times_pow(m - 1 - t, bb_re, bb_im, 1) for t in range(m)]
    x = jnp.stack([jnp.stack(p, axis=0) for p in b_parts], axis=0)
    x = x.reshape(m, 2, nb, groups_per_block, SSM_STATE, SSM_GROUP)
    wb_compact = x.transpose(2, 0, 3, 5, 1, 4).reshape(nb, m * LANES, 2 * SSM_STATE)

    c_parts = [times_pow(t + 1, ct_re, ct_im, 1) for t in range(m)]
    x = jnp.stack([jnp.stack([p[0], -p[1]], axis=0) for p in c_parts], axis=0)
    x = x.reshape(m, 2, nb, PAIRS_PER_BLOCK, LANES, SSM_GROUP)
    wc_compact = x.transpose(2, 3, 1, 4, 0, 5).reshape(nb, BLOCK_STATE, m * SSM_GROUP)

    kern = []
    for k in range(m):
        n_re, n_im = times_pow(k, ct_re, ct_im, 1)
        kern.append(jnp.einsum('gpo,gpi->gio', n_re, bb_re, precision=hi)
                    - jnp.einsum('gpo,gpi->gio', n_im, bb_im, precision=hi))
    zero = jnp.zeros_like(kern[0])
    x = jnp.stack([jnp.stack([kern[t - t0] if t0 <= t else zero for t in range(m)], axis=0)
                   for t0 in range(m)], axis=0)
    x = x.reshape(m, m, nb, groups_per_block, SSM_GROUP, SSM_GROUP)
    wt_compact = x.transpose(2, 0, 3, 4, 1, 5).reshape(nb, m * LANES, m * SSM_GROUP)

    wb, wt, wc = _expand_operands(wb_compact, wt_compact, wc_compact)

    ar = pow_re[m].reshape(N_SG, PAIRS_PER_SG, LANES)
    ai = pow_im[m].reshape(N_SG, PAIRS_PER_SG, LANES)
    return wb, wt, wc, ar, ai


def kernel(x, norm_ffn1_g, w_ffn1_up, w_ffn1_down, norm_mix_g, w_in, ssm_a_re, ssm_a_im, ssm_log_dt, ssm_b_re, ssm_b_im, ssm_c_re, ssm_c_im, ssm_d, w_ssm_glu, conv_dw_w, conv_dw_b, conv_ln_g, conv_ln_b, w_conv_pw, w_out, norm_ffn2_g, w_ffn2_up, w_ffn2_down, norm_final_g):
    bsz, seq, d = x.shape
    depth = w_in.shape[0]
    assert d == D_MODEL and w_ffn1_up.shape[-1] == 2 * D_FF
    tm = 256
    ts = 1024
    tm_mix = 512
    assert seq % tm == 0 and seq % ts == 0 and (bsz * seq) % tm_mix == 0
    row = lambda a: a.reshape(1, -1).astype(F32)

    x2d = x.reshape(bsz * seq, d)
    for l in range(depth):
        conv_w = jnp.pad(conv_dw_w[l].reshape(CONV_WIDTH, d), ((0, 1), (0, 0)))
        x1, u, gl, c = _ffn_conv(
            x2d, row(norm_ffn1_g[l]), w_ffn1_up[l].astype(BF16), w_ffn1_down[l].astype(BF16),
            row(norm_mix_g[l]), w_in[l].astype(BF16), conv_w, row(conv_dw_b[l]),
            row(conv_ln_g[l]), row(conv_ln_b[l]), tm, seq // tm)
        wb, wt, wc, ar, ai = _ssm_params(ssm_a_re[l], ssm_a_im[l], ssm_log_dt[l], ssm_b_re[l],
                                         ssm_b_im[l], ssm_c_re[l], ssm_c_im[l])
        d_chunk = jnp.tile(row(ssm_d[l]), (1, SSM_CHUNK))
        chunk_rows = seq // SSM_CHUNK
        y = _ssm_chunk(u.reshape(bsz, chunk_rows, SSM_CHUNK * d), wb, wt, wc, ar, ai, d_chunk, ts)
        x2d = _mix_ffn(x1, c, gl, y.reshape(bsz * chunk_rows, SSM_CHUNK * d),
                       w_ssm_glu[l].astype(BF16),
                       w_conv_pw[l].astype(BF16), w_out[l].astype(BF16), row(norm_ffn2_g[l]),
                       w_ffn2_up[l].astype(BF16), w_ffn2_down[l].astype(BF16),
                       row(norm_final_g), tm_mix, final_norm=(l == depth - 1))
    return x2d.reshape(bsz, seq, d)
```

```python
import functools

import jax
import jax.numpy as jnp
import numpy as np
from jax import lax
from jax.experimental import pallas as pl
from jax.experimental.pallas import tpu as pltpu

D_MODEL = 1024
D_FF = 2816
SSM_GROUP = 16
N_GROUPS = 64
SSM_STATE = 64
CONV_WIDTH = 31
RMS_EPS = 1e-6
LN_EPS = 1e-5
FFN_SCALE = 0.5

LANES = 128
SUBLANES = 8
MXU_DIM = 256
VMEM_LIMIT_BYTES = 56 * 1024 * 1024

PAIRS_PER_SG = SUBLANES
N_SG = N_GROUPS // (2 * PAIRS_PER_SG)
SSM_CHUNK = 4
N_LANE_BLOCKS = D_MODEL // LANES
PAIRS_PER_BLOCK = LANES // (2 * SSM_GROUP)
BLOCK_STATE = 2 * LANES * PAIRS_PER_BLOCK
STATE_PITCH = 12
N_SLABS = D_MODEL // LANES
CONV_HALO = 32
CONV_CHAINS = 2

FFN_CHUNKS = ((0, 1536), (1536, D_FF))

BF16 = jnp.bfloat16
F32 = jnp.float32


def _dot(a, b):
    return jnp.dot(a, b, preferred_element_type=F32)


def _rms_norm(x, g):
    return x * lax.rsqrt(jnp.mean(x * x, axis=-1, keepdims=True) + RMS_EPS) * g


def _sigmoid(x):
    return 1.0 / (1.0 + jnp.exp(-x))


def _zero_of(x):
    bits = lax.bitcast_convert_type(x, jnp.uint32)
    return lax.bitcast_convert_type((bits >> 16) >> 16, F32)


def _swiglu(h, wup_ref, wdn_ref):
    acc = None
    for c0, c1 in FFN_CHUNKS:
        gate = _dot(h, wup_ref[:, c0:c1])
        up = _dot(h, wup_ref[:, D_FF + c0:D_FF + c1])
        act = (gate * _sigmoid(gate) * up).astype(BF16)
        part = _dot(act, wdn_ref[c0:c1, :])
        acc = part if acc is None else acc + part
    return acc


def _const_spec(shape):
    nd = len(shape)
    return pl.BlockSpec(shape, lambda *_: (0,) * nd, pipeline_mode=pl.Buffered(1))


def _ffn_conv_kernel(x_ref, g1_ref, wup_ref, wdn_ref, gm_ref, win_ref,
                     cw_ref, cb_ref, lng_ref, lnb_ref,
                     x1_ref, u_ref, gl_ref, c_ref, z_ref, cv_ref, us_ref, *, tm, blocks_per_seq):
    g = pl.program_id(0)

    @pl.when(g == 0)
    def _():
        z_ref[...] = jnp.zeros_like(z_ref)

    base = CONV_HALO - (CONV_WIDTH - 1)
    tiles = [(c, r) for c in range(N_SLABS) for r in range(tm // SUBLANES)]
    tiles_per_chain = len(tiles) // CONV_CHAINS
    prev = [None] * CONV_CHAINS
    for i in range(tiles_per_chain):
        for ch in range(CONV_CHAINS):
            c, r = tiles[ch * tiles_per_chain + i]
            lanes = slice(c * LANES, (c + 1) * LANES)
            acc = jnp.broadcast_to(cb_ref[:, lanes], (SUBLANES, LANES))
            if prev[ch] is not None:
                acc = acc + _zero_of(prev[ch])
            for k in range(CONV_WIDTH):
                rows = pl.ds(base + k + r * SUBLANES, SUBLANES, stride=1)
                acc = acc + cw_ref[k:k + 1, lanes] * z_ref[c, rows, :]
            cv_ref[r * SUBLANES:(r + 1) * SUBLANES, lanes] = acc
            prev[ch] = acc

    cv = cv_ref[...]
    mu = jnp.mean(cv, axis=-1, keepdims=True)
    xc = cv - mu
    var = jnp.mean(xc * xc, axis=-1, keepdims=True)
    ln = xc * lax.rsqrt(var + LN_EPS) * lng_ref[...] + lnb_ref[...]
    c_ref[...] = (ln * _sigmoid(ln)).astype(c_ref.dtype)

    x = x_ref[...]
    h = _rms_norm(x, g1_ref[...]).astype(BF16)
    x1 = x + FFN_SCALE * _swiglu(h, wup_ref, wdn_ref)
    x1_ref[...] = x1
    h2 = _rms_norm(x1, gm_ref[...]).astype(BF16)
    u = _dot(h2, win_ref[:, 0:D_MODEL])
    for s in range(N_SLABS):
        us_ref[s] = u[:, s * LANES:(s + 1) * LANES]
    for t in range(SSM_CHUNK):
        for s in range(N_SLABS):
            lanes = slice(t * D_MODEL + s * LANES, t * D_MODEL + (s + 1) * LANES)
            rows = pl.ds(t, tm // SSM_CHUNK, stride=SSM_CHUNK)
            u_ref[:, lanes] = us_ref[s, rows, :].astype(u_ref.dtype)
    gl_ref[...] = _dot(h2, win_ref[:, 3 * D_MODEL:5 * D_MODEL]).astype(gl_ref.dtype)
    v = _dot(h2, win_ref[:, D_MODEL:3 * D_MODEL])
    z = v[:, :D_MODEL] * _sigmoid(v[:, D_MODEL:])

    seq_start = (g % blocks_per_seq) == 0
    for c in range(N_SLABS):
        tail = z_ref[c, tm:tm + CONV_HALO, :]
        z_ref[c, 0:CONV_HALO, :] = jnp.where(seq_start, 0.0, tail)
        z_ref[c, CONV_HALO:CONV_HALO + tm, :] = z[:, c * LANES:(c + 1) * LANES]


def _ffn_conv(x2d, g1, wup, wdn, gm, win, cw, cb, lng, lnb, tm, blocks_per_seq):
    n_tok = x2d.shape[0]
    n_blk = n_tok // tm
    cur = lambda w: pl.BlockSpec((tm, w), lambda i: (jnp.minimum(i, n_blk - 1), 0))
    prev = lambda w: pl.BlockSpec((tm, w), lambda i: (jnp.maximum(i - 1, 0), 0))
    consts = (g1, wup, wdn, gm, win, cw, cb, lng, lnb)
    return pl.pallas_call(
        functools.partial(_ffn_conv_kernel, tm=tm, blocks_per_seq=blocks_per_seq),
        grid=(n_blk + 1,),
        in_specs=[cur(D_MODEL)] + [_const_spec(c.shape) for c in consts],
        out_specs=[cur(D_MODEL),
                   pl.BlockSpec((tm // SSM_CHUNK, SSM_CHUNK * D_MODEL),
                                lambda i: (jnp.minimum(i, n_blk - 1), 0)),
                   cur(2 * D_MODEL), prev(D_MODEL)],
        out_shape=[jax.ShapeDtypeStruct((n_tok, D_MODEL), F32),
                   jax.ShapeDtypeStruct((n_tok // SSM_CHUNK, SSM_CHUNK * D_MODEL), BF16),
                   jax.ShapeDtypeStruct((n_tok, 2 * D_MODEL), BF16),
                   jax.ShapeDtypeStruct((n_tok, D_MODEL), BF16)],
        scratch_shapes=[pltpu.VMEM((N_SLABS, CONV_HALO + tm, LANES), F32),
                        pltpu.VMEM((tm, D_MODEL), F32),
                        pltpu.VMEM((N_SLABS, tm, LANES), F32)],
        compiler_params=pltpu.CompilerParams(
            dimension_semantics=("arbitrary",), vmem_limit_bytes=VMEM_LIMIT_BYTES),
        name="ffn_conv",
    )(x2d, *consts)


def _ssm_chunk_kernel(u_ref, wb_ref, wt_ref, wc_ref, ar_ref, ai_ref, d_ref,
                      y_ref, sre_ref, sim_ref, yt_ref, carry_ref, *, nc):
    @pl.when(pl.program_id(1) == 0)
    def _():
        carry_ref[...] = jnp.zeros_like(carry_ref)

    def chunk_lanes(blk):
        return [slice(t * D_MODEL + blk * LANES, t * D_MODEL + (blk + 1) * LANES)
                for t in range(SSM_CHUNK)]

    def pair_rows(blk, q):
        pair = blk * PAIRS_PER_BLOCK + q
        return pair // PAIRS_PER_SG, pl.ds(pair % PAIRS_PER_SG, nc, stride=STATE_PITCH)

    for blk in range(N_LANE_BLOCKS):
        lanes = chunk_lanes(blk)
        u_blk = jnp.concatenate([u_ref[0, :, l] for l in lanes], axis=1)
        wbu = _dot(u_blk, wb_ref[blk])
        for q in range(PAIRS_PER_BLOCK):
            sg, rows = pair_rows(blk, q)
            c0 = 2 * LANES * q
            sre_ref[sg, rows, :] = wbu[:, c0:c0 + LANES]
            sim_ref[sg, rows, :] = wbu[:, c0 + LANES:c0 + 2 * LANES]
        wtu = _dot(u_blk, wt_ref[blk])
        for t, l in enumerate(lanes):
            yt_ref[:, l] = wtu[:, t * LANES:(t + 1) * LANES] + d_ref[:, l] * u_ref[0, :, l].astype(F32)

    ar = [ar_ref[sg] for sg in range(N_SG)]
    ai = [ai_ref[sg] for sg in range(N_SG)]

    def step(c, carry):
        rows = pl.ds(c * STATE_PITCH, PAIRS_PER_SG, stride=1)
        new = []
        for sg in range(N_SG):
            sr, si = carry[2 * sg], carry[2 * sg + 1]
            nr = ar[sg] * sr - ai[sg] * si + sre_ref[sg, rows, :]
            ni = ar[sg] * si + ai[sg] * sr + sim_ref[sg, rows, :]
            sre_ref[sg, rows, :] = sr
            sim_ref[sg, rows, :] = si
            new += [nr, ni]
        return tuple(new)

    init = tuple(carry_ref[i] for i in range(2 * N_SG))
    final = lax.fori_loop(0, nc, step, init, unroll=8)
    for i in range(2 * N_SG):
        carry_ref[i] = final[i]

    for blk in range(N_LANE_BLOCKS):
        parts = []
        for q in range(PAIRS_PER_BLOCK):
            sg, rows = pair_rows(blk, q)
            parts += [sre_ref[sg, rows, :].astype(BF16), sim_ref[sg, rows, :].astype(BF16)]
        wcs = _dot(jnp.concatenate(parts, axis=1), wc_ref[blk])
        for t, l in enumerate(chunk_lanes(blk)):
            y_ref[0, :, l] = (yt_ref[:, l] + wcs[:, t * LANES:(t + 1) * LANES]).astype(y_ref.dtype)


def _ssm_chunk(u_chunks, wb, wt, wc, ar, ai, d, tokens_per_step):
    bsz, n_chunks, width = u_chunks.shape
    nc = tokens_per_step // SSM_CHUNK
    consts = (wb, wt, wc, ar, ai, d)
    blk = pl.BlockSpec((1, nc, width), lambda b, i: (b, i, 0))
    state = pltpu.VMEM((N_SG, nc * STATE_PITCH, LANES), F32)
    return pl.pallas_call(
        functools.partial(_ssm_chunk_kernel, nc=nc),
        grid=(bsz, n_chunks // nc),
        in_specs=[blk] + [_const_spec(c.shape) for c in consts],
        out_specs=blk,
        out_shape=jax.ShapeDtypeStruct(u_chunks.shape, BF16),
        scratch_shapes=[state, state, pltpu.VMEM((nc, width), F32),
                        pltpu.VMEM((2 * N_SG, SUBLANES, LANES), F32)],
        compiler_params=pltpu.CompilerParams(
            dimension_semantics=("arbitrary", "arbitrary"),
            vmem_limit_bytes=VMEM_LIMIT_BYTES),
        name="ssm_chunk",
    )(u_chunks, *consts)


def _mix_ffn_kernel(x1_ref, c_ref, gl_ref, y_ref, wglu_ref, wpw_ref, wout_ref, g2_ref,
                    wup_ref, wdn_ref, gf_ref, o_ref, ys_ref, *, tm, final_norm):
    for t in range(SSM_CHUNK):
        for s in range(N_SLABS):
            lanes = slice(t * D_MODEL + s * LANES, t * D_MODEL + (s + 1) * LANES)
            rows = pl.ds(t, tm // SSM_CHUNK, stride=SSM_CHUNK)
            ys_ref[s, rows, :] = y_ref[:, lanes].astype(F32)
    y = jnp.concatenate([ys_ref[s] for s in range(N_SLABS)], axis=1)
    glu = _dot(jax.nn.gelu(y).astype(BF16), wglu_ref[...])
    ya = glu[:, :D_MODEL] * _sigmoid(glu[:, D_MODEL:])
    yb = _dot(c_ref[...], wpw_ref[...])
    gates = _sigmoid(gl_ref[...].astype(F32))
    mix = gates[:, :D_MODEL] * ya + gates[:, D_MODEL:] * yb
    x2 = x1_ref[...] + _dot(mix.astype(BF16), wout_ref[...])
    h = _rms_norm(x2, g2_ref[...]).astype(BF16)
    x3 = x2 + FFN_SCALE * _swiglu(h, wup_ref, wdn_ref)
    o_ref[...] = _rms_norm(x3, gf_ref[...]) if final_norm else x3


def _mix_ffn(x1, c, gl, y, wglu, wpw, wout, g2, wup, wdn, gf, tm, final_norm):
    n_tok = x1.shape[0]
    row = lambda w: pl.BlockSpec((tm, w), lambda i: (i, 0))
    consts = (wglu, wpw, wout, g2, wup, wdn, gf)
    y_spec = pl.BlockSpec((tm // SSM_CHUNK, SSM_CHUNK * D_MODEL), lambda i: (i, 0))
    return pl.pallas_call(
        functools.partial(_mix_ffn_kernel, tm=tm, final_norm=final_norm),
        grid=(n_tok // tm,),
        in_specs=[row(D_MODEL), row(D_MODEL), row(2 * D_MODEL), y_spec]
                 + [_const_spec(c.shape) for c in consts],
        out_specs=row(D_MODEL),
        out_shape=jax.ShapeDtypeStruct((n_tok, D_MODEL), F32),
        scratch_shapes=[pltpu.VMEM((N_SLABS, tm, LANES), F32)],
        compiler_params=pltpu.CompilerParams(
            dimension_semantics=("arbitrary",), vmem_limit_bytes=VMEM_LIMIT_BYTES),
        name="mix_ffn",
    )(x1, c, gl, y, *consts)


def _expand_kernel(b_ref, t_ref, c_ref, eb_ref, et_ref, mb_ref, mt_ref, mc_ref,
                   wb_ref, wt_ref, wc_ref):
    wb_ref[0] = (_dot(b_ref[0].astype(BF16), eb_ref[...]) * mb_ref[...]).astype(BF16)
    wt_ref[0] = (_dot(t_ref[0].astype(BF16), et_ref[...]) * mt_ref[...]).astype(BF16)
    wc_ref[0] = (_dot(c_ref[0].astype(BF16), et_ref[...]) * mc_ref[...]).astype(BF16)


def _expand_operands(wb_compact, wt_compact, wc_compact):
    m = SSM_CHUNK
    idx = lambda n: np.arange(n)
    src, dst = idx(2 * SSM_STATE)[:, None], idx(BLOCK_STATE)[None, :]
    spread_b = ((src // SSM_STATE == dst // LANES % 2)
                & (src % SSM_STATE == dst % SSM_STATE)).astype(BF16)
    row_group = idx(m * LANES)[:, None] % LANES // SSM_GROUP
    mask_b = ((row_group // 2 == dst // (2 * LANES))
              & (row_group % 2 == dst % LANES // SSM_STATE)).astype(F32)
    src, dst = idx(m * SSM_GROUP)[:, None], idx(m * LANES)[None, :]
    spread_t = ((src // SSM_GROUP == dst // LANES)
                & (src % SSM_GROUP == dst % SSM_GROUP)).astype(BF16)
    mask_t = (row_group == dst % LANES // SSM_GROUP).astype(F32)
    row = idx(BLOCK_STATE)[:, None]
    row_group = row // (2 * LANES) * 2 + row % LANES // SSM_STATE
    mask_c = (row_group == dst % LANES // SSM_GROUP).astype(F32)

    consts = (spread_b, spread_t, mask_b, mask_t, mask_c)
    per_block = lambda a: pl.BlockSpec((1,) + a.shape[1:], lambda i: (i, 0, 0))
    shapes = [(N_LANE_BLOCKS, m * LANES, BLOCK_STATE), (N_LANE_BLOCKS, m * LANES, m * LANES),
              (N_LANE_BLOCKS, BLOCK_STATE, m * LANES)]
    return pl.pallas_call(
        _expand_kernel,
        grid=(N_LANE_BLOCKS,),
        in_specs=[per_block(wb_compact), per_block(wt_compact), per_block(wc_compact)]
                 + [pl.BlockSpec(c.shape, lambda i: (0, 0)) for c in consts],
        out_specs=[pl.BlockSpec((1,) + s[1:], lambda i: (i, 0, 0)) for s in shapes],
        out_shape=[jax.ShapeDtypeStruct(s, BF16) for s in shapes],
        compiler_params=pltpu.CompilerParams(dimension_semantics=("arbitrary",)),
        name="expand_operands",
    )(wb_compact, wt_compact, wc_compact, *consts)


def _ssm_params(a_re, a_im, log_dt, b_re, b_im, c_re, c_im):
    hi = lax.Precision.HIGHEST
    m = SSM_CHUNK
    dt = jnp.exp(log_dt)[:, None]
    mag = jnp.exp(dt * a_re)
    abar_re = mag * jnp.cos(dt * a_im)
    abar_im = mag * jnp.sin(dt * a_im)
    den = a_re * a_re + a_im * a_im
    nr = abar_re - 1.0
    ni = abar_im
    f_re = (nr * a_re + ni * a_im) / den
    f_im = (ni * a_re - nr * a_im) / den
    bb_re = f_re[..., None] * b_re - f_im[..., None] * b_im
    bb_im = f_re[..., None] * b_im + f_im[..., None] * b_re

    pow_re, pow_im = [jnp.ones_like(abar_re)], [jnp.zeros_like(abar_re)]
    for _ in range(m):
        pr, pi = pow_re[-1], pow_im[-1]
        pow_re.append(pr * abar_re - pi * abar_im)
        pow_im.append(pr * abar_im + pi * abar_re)
    def times_pow(ks, x_re, x_im):
        pr = jnp.stack([pow_re[k] for k in ks], axis=0)[..., None]
        pi = jnp.stack([pow_im[k] for k in ks], axis=0)[..., None]
        return x_re * pr - x_im * pi, x_re * pi + x_im * pr

    ct_re, ct_im = jnp.swapaxes(c_re, 1, 2), jnp.swapaxes(c_im, 1, 2)

    groups_per_block = LANES // SSM_GROUP
    nb = N_LANE_BLOCKS

    x = jnp.stack(times_pow(range(m - 1, -1, -1), bb_re, bb_im), axis=1)
    x = x.reshape(m, 2, nb, groups_per_block, SSM_STATE, SSM_GROUP)
    wb_compact = x.transpose(2, 0, 3, 5, 1, 4).reshape(nb, m * LANES, 2 * SSM_STATE)

    n_re, n_im = times_pow(range(1, m + 1), ct_re, ct_im)
    x = jnp.stack([n_re, -n_im], axis=1)
    x = x.reshape(m, 2, nb, PAIRS_PER_BLOCK, LANES, SSM_GROUP)
    wc_compact = x.transpose(2, 3, 1, 4, 0, 5).reshape(nb, BLOCK_STATE, m * SSM_GROUP)

    n_re, n_im = times_pow(range(m), ct_re, ct_im)
    kern = (jnp.einsum('kgpo,gpi->kgio', n_re, bb_re, precision=hi)
            - jnp.einsum('kgpo,gpi->kgio', n_im, bb_im, precision=hi))
    zero = jnp.zeros_like(kern[0])
    x = jnp.stack([jnp.stack([kern[t - t0] if t0 <= t else zero for t in range(m)], axis=0)
                   for t0 in range(m)], axis=0)
    x = x.reshape(m, m, nb, groups_per_block, SSM_GROUP, SSM_GROUP)
    wt_compact = x.transpose(2, 0, 3, 4, 1, 5).reshape(nb, m * LANES, m * SSM_GROUP)

    wb, wt, wc = _expand_operands(wb_compact, wt_compact, wc_compact)

    ar = pow_re[m].reshape(N_SG, PAIRS_PER_SG, LANES)
    ai = pow_im[m].reshape(N_SG, PAIRS_PER_SG, LANES)
    return wb, wt, wc, ar, ai


def kernel(x, norm_ffn1_g, w_ffn1_up, w_ffn1_down, norm_mix_g, w_in, ssm_a_re, ssm_a_im, ssm_log_dt, ssm_b_re, ssm_b_im, ssm_c_re, ssm_c_im, ssm_d, w_ssm_glu, conv_dw_w, conv_dw_b, conv_ln_g, conv_ln_b, w_conv_pw, w_out, norm_ffn2_g, w_ffn2_up, w_ffn2_down, norm_final_g):
    bsz, seq, d = x.shape
    depth = w_in.shape[0]
    assert d == D_MODEL and w_ffn1_up.shape[-1] == 2 * D_FF
    tm = 256
    ts = 1024
    tm_mix = 512
    assert seq % tm == 0 and seq % ts == 0 and (bsz * seq) % tm_mix == 0
    row = lambda a: a.reshape(1, -1).astype(F32)

    x2d = x.reshape(bsz * seq, d)
    for l in range(depth):
        conv_w = jnp.pad(conv_dw_w[l].reshape(CONV_WIDTH, d), ((0, 1), (0, 0)))
        x1, u, gl, c = _ffn_conv(
            x2d, row(norm_ffn1_g[l]), w_ffn1_up[l].astype(BF16), w_ffn1_down[l].astype(BF16),
            row(norm_mix_g[l]), w_in[l].astype(BF16), conv_w, row(conv_dw_b[l]),
            row(conv_ln_g[l]), row(conv_ln_b[l]), tm, seq // tm)
        wb, wt, wc, ar, ai = _ssm_params(ssm_a_re[l], ssm_a_im[l], ssm_log_dt[l], ssm_b_re[l],
                                         ssm_b_im[l], ssm_c_re[l], ssm_c_im[l])
        d_chunk = jnp.tile(row(ssm_d[l]), (1, SSM_CHUNK))
        chunk_rows = seq // SSM_CHUNK
        y = _ssm_chunk(u.reshape(bsz, chunk_rows, SSM_CHUNK * d), wb, wt, wc, ar, ai, d_chunk, ts)
        x2d = _mix_ffn(x1, c, gl, y.reshape(bsz * chunk_rows, SSM_CHUNK * d),
                       w_ssm_glu[l].astype(BF16),
                       w_conv_pw[l].astype(BF16), w_out[l].astype(BF16), row(norm_ffn2_g[l]),
                       w_ffn2_up[l].astype(BF16), w_ffn2_down[l].astype(BF16),
                       row(norm_final_g), tm_mix, final_norm=(l == depth - 1))
    return x2d.reshape(bsz, seq, d)
```

```python
import functools

import jax
import jax.numpy as jnp
import numpy as np
from jax import lax
from jax.experimental import pallas as pl
from jax.experimental.pallas import tpu as pltpu

D_MODEL = 1024
D_FF = 2816
SSM_GROUP = 16
N_GROUPS = 64
SSM_STATE = 64
CONV_WIDTH = 31
RMS_EPS = 1e-6
LN_EPS = 1e-5
FFN_SCALE = 0.5

LANES = 128
SUBLANES = 8
MXU_DIM = 256
VMEM_LIMIT_BYTES = 56 * 1024 * 1024

PAIRS_PER_SG = SUBLANES
N_SG = N_GROUPS // (2 * PAIRS_PER_SG)
SSM_CHUNK = 4
N_LANE_BLOCKS = D_MODEL // LANES
PAIRS_PER_BLOCK = LANES // (2 * SSM_GROUP)
BLOCK_STATE = 2 * LANES * PAIRS_PER_BLOCK
STATE_PITCH = 12
N_SLABS = D_MODEL // LANES
CONV_HALO = 32
CONV_CHAINS = 2

FFN_CHUNKS = ((0, 1536), (1536, D_FF))

BF16 = jnp.bfloat16
F32 = jnp.float32


def _dot(a, b):
    return jnp.dot(a, b, preferred_element_type=F32)


def _rms_norm(x, g):
    return x * lax.rsqrt(jnp.mean(x * x, axis=-1, keepdims=True) + RMS_EPS) * g


def _sigmoid(x):
    return 1.0 / (1.0 + jnp.exp(-x))


def _zero_of(x):
    bits = lax.bitcast_convert_type(x, jnp.uint32)
    return lax.bitcast_convert_type((bits >> 16) >> 16, F32)


def _swiglu(h, wup_ref, wdn_ref):
    acc = None
    for c0, c1 in FFN_CHUNKS:
        gate = _dot(h, wup_ref[:, c0:c1])
        up = _dot(h, wup_ref[:, D_FF + c0:D_FF + c1])
        act = (gate * _sigmoid(gate) * up).astype(BF16)
        part = _dot(act, wdn_ref[c0:c1, :])
        acc = part if acc is None else acc + part
    return acc


def _const_spec(shape):
    nd = len(shape)
    return pl.BlockSpec(shape, lambda *_: (0,) * nd, pipeline_mode=pl.Buffered(1))


def _ffn_conv_kernel(x_ref, g1_ref, wup_ref, wdn_ref, gm_ref, win_ref,
                     cw_ref, cb_ref, lng_ref, lnb_ref,
                     x1_ref, u_ref, gl_ref, c_ref, z_ref, cv_ref, us_ref, *, tm, blocks_per_seq):
    g = pl.program_id(0)

    @pl.when(g == 0)
    def _():
        z_ref[...] = jnp.zeros_like(z_ref)

    base = CONV_HALO - (CONV_WIDTH - 1)
    tiles = [(c, r) for c in range(N_SLABS) for r in range(tm // SUBLANES)]
    tiles_per_chain = len(tiles) // CONV_CHAINS
    prev = [None] * CONV_CHAINS
    for i in range(tiles_per_chain):
        for ch in range(CONV_CHAINS):
            c, r = tiles[ch * tiles_per_chain + i]
            lanes = slice(c * LANES, (c + 1) * LANES)
            acc = jnp.broadcast_to(cb_ref[:, lanes], (SUBLANES, LANES))
            if prev[ch] is not None:
                acc = acc + _zero_of(prev[ch])
            for k in range(CONV_WIDTH):
                rows = pl.ds(base + k + r * SUBLANES, SUBLANES, stride=1)
                acc = acc + cw_ref[k:k + 1, lanes] * z_ref[c, rows, :]
            cv_ref[r * SUBLANES:(r + 1) * SUBLANES, lanes] = acc
            prev[ch] = acc

    cv = cv_ref[...]
    mu = jnp.mean(cv, axis=-1, keepdims=True)
    xc = cv - mu
    var = jnp.mean(xc * xc, axis=-1, keepdims=True)
    ln = xc * lax.rsqrt(var + LN_EPS) * lng_ref[...] + lnb_ref[...]
    c_ref[...] = (ln * _sigmoid(ln)).astype(c_ref.dtype)

    x = x_ref[...]
    h = _rms_norm(x, g1_ref[...]).astype(BF16)
    x1 = x + FFN_SCALE * _swiglu(h, wup_ref, wdn_ref)
    x1_ref[...] = x1
    h2 = _rms_norm(x1, gm_ref[...]).astype(BF16)
    u = _dot(h2, win_ref[:, 0:D_MODEL])
    for s in range(N_SLABS):
        us_ref[s] = u[:, s * LANES:(s + 1) * LANES]
    for t in range(SSM_CHUNK):
        for s in range(N_SLABS):
            lanes = slice(t * D_MODEL + s * LANES, t * D_MODEL + (s + 1) * LANES)
            rows = pl.ds(t, tm // SSM_CHUNK, stride=SSM_CHUNK)
            u_ref[:, lanes] = us_ref[s, rows, :].astype(u_ref.dtype)
    gl_ref[...] = _dot(h2, win_ref[:, 3 * D_MODEL:5 * D_MODEL]).astype(gl_ref.dtype)
    v = _dot(h2, win_ref[:, D_MODEL:3 * D_MODEL])
    z = v[:, :D_MODEL] * _sigmoid(v[:, D_MODEL:])

    seq_start = (g % blocks_per_seq) == 0
    for c in range(N_SLABS):
        tail = z_ref[c, tm:tm + CONV_HALO, :]
        z_ref[c, 0:CONV_HALO, :] = jnp.where(seq_start, 0.0, tail)
        z_ref[c, CONV_HALO:CONV_HALO + tm, :] = z[:, c * LANES:(c + 1) * LANES]


def _ffn_conv(x2d, g1, wup, wdn, gm, win, cw, cb, lng, lnb, tm, blocks_per_seq):
    n_tok = x2d.shape[0]
    n_blk = n_tok // tm
    cur = lambda w: pl.BlockSpec((tm, w), lambda i: (jnp.minimum(i, n_blk - 1), 0))
    prev = lambda w: pl.BlockSpec((tm, w), lambda i: (jnp.maximum(i - 1, 0), 0))
    consts = (g1, wup, wdn, gm, win, cw, cb, lng, lnb)
    return pl.pallas_call(
        functools.partial(_ffn_conv_kernel, tm=tm, blocks_per_seq=blocks_per_seq),
        grid=(n_blk + 1,),
        in_specs=[cur(D_MODEL)] + [_const_spec(c.shape) for c in consts],
        out_specs=[cur(D_MODEL),
                   pl.BlockSpec((tm // SSM_CHUNK, SSM_CHUNK * D_MODEL),
                                lambda i: (jnp.minimum(i, n_blk - 1), 0)),
                   cur(2 * D_MODEL), prev(D_MODEL)],
        out_shape=[jax.ShapeDtypeStruct((n_tok, D_MODEL), F32),
                   jax.ShapeDtypeStruct((n_tok // SSM_CHUNK, SSM_CHUNK * D_MODEL), BF16),
                   jax.ShapeDtypeStruct((n_tok, 2 * D_MODEL), BF16),
                   jax.ShapeDtypeStruct((n_tok, D_MODEL), BF16)],
        scratch_shapes=[pltpu.VMEM((N_SLABS, CONV_HALO + tm, LANES), F32),
                        pltpu.VMEM((tm, D_MODEL), F32),
                        pltpu.VMEM((N_SLABS, tm, LANES), F32)],
        compiler_params=pltpu.CompilerParams(
            dimension_semantics=("arbitrary",), vmem_limit_bytes=VMEM_LIMIT_BYTES),
        name="ffn_conv",
    )(x2d, *consts)


def _ssm_chunk_kernel(u_ref, wb_ref, wt_ref, wc_ref, ar_ref, ai_ref, d_ref,
                      y_ref, sre_ref, sim_ref, yt_ref, carry_ref, *, nc):
    @pl.when(pl.program_id(1) == 0)
    def _():
        carry_ref[...] = jnp.zeros_like(carry_ref)

    def chunk_lanes(blk):
        return [slice(t * D_MODEL + blk * LANES, t * D_MODEL + (blk + 1) * LANES)
                for t in range(SSM_CHUNK)]

    def pair_rows(blk, q):
        pair = blk * PAIRS_PER_BLOCK + q
        return pair // PAIRS_PER_SG, pl.ds(pair % PAIRS_PER_SG, nc, stride=STATE_PITCH)

    for blk in range(N_LANE_BLOCKS):
        lanes = chunk_lanes(blk)
        u_blk = jnp.concatenate([u_ref[0, :, l] for l in lanes], axis=1)
        wbu = _dot(u_blk, wb_ref[blk])
        for q in range(PAIRS_PER_BLOCK):
            sg, rows = pair_rows(blk, q)
            c0 = 2 * LANES * q
            sre_ref[sg, rows, :] = wbu[:, c0:c0 + LANES]
            sim_ref[sg, rows, :] = wbu[:, c0 + LANES:c0 + 2 * LANES]
        wtu = _dot(u_blk, wt_ref[blk])
        for t, l in enumerate(lanes):
            yt_ref[:, l] = wtu[:, t * LANES:(t + 1) * LANES] + d_ref[:, l] * u_ref[0, :, l].astype(F32)

    ar = [ar_ref[sg] for sg in range(N_SG)]
    ai = [ai_ref[sg] for sg in range(N_SG)]

    def step(c, carry):
        rows = pl.ds(c * STATE_PITCH, PAIRS_PER_SG, stride=1)
        new = []
        for sg in range(N_SG):
            sr, si = carry[2 * sg], carry[2 * sg + 1]
            nr = ar[sg] * sr - ai[sg] * si + sre_ref[sg, rows, :]
            ni = ar[sg] * si + ai[sg] * sr + sim_ref[sg, rows, :]
            sre_ref[sg, rows, :] = sr
            sim_ref[sg, rows, :] = si
            new += [nr, ni]
        return tuple(new)

    init = tuple(carry_ref[i] for i in range(2 * N_SG))
    final = lax.fori_loop(0, nc, step, init, unroll=8)
    for i in range(2 * N_SG):
        carry_ref[i] = final[i]

    for blk in range(N_LANE_BLOCKS):
        parts = []
        for q in range(PAIRS_PER_BLOCK):
            sg, rows = pair_rows(blk, q)
            parts += [sre_ref[sg, rows, :].astype(BF16), sim_ref[sg, rows, :].astype(BF16)]
        wcs = _dot(jnp.concatenate(parts, axis=1), wc_ref[blk])
        for t, l in enumerate(chunk_lanes(blk)):
            y_ref[0, :, l] = (yt_ref[:, l] + wcs[:, t * LANES:(t + 1) * LANES]).astype(y_ref.dtype)


def _ssm_chunk(u_chunks, wb, wt, wc, ar, ai, d, tokens_per_step):
    bsz, n_chunks, width = u_chunks.shape
    nc = tokens_per_step // SSM_CHUNK
    consts = (wb, wt, wc, ar, ai, d)
    blk = pl.BlockSpec((1, nc, width), lambda b, i: (b, i, 0))
    state = pltpu.VMEM((N_SG, nc * STATE_PITCH, LANES), F32)
    return pl.pallas_call(
        functools.partial(_ssm_chunk_kernel, nc=nc),
        grid=(bsz, n_chunks // nc),
        in_specs=[blk] + [_const_spec(c.shape) for c in consts],
        out_specs=blk,
        out_shape=jax.ShapeDtypeStruct(u_chunks.shape, BF16),
        scratch_shapes=[state, state, pltpu.VMEM((nc, width), F32),
                        pltpu.VMEM((2 * N_SG, SUBLANES, LANES), F32)],
        compiler_params=pltpu.CompilerParams(
            dimension_semantics=("arbitrary", "arbitrary"),
            vmem_limit_bytes=VMEM_LIMIT_BYTES),
        name="ssm_chunk",
    )(u_chunks, *consts)


def _mix_ffn_kernel(x1_ref, c_ref, gl_ref, y_ref, wglu_ref, wpw_ref, wout_ref, g2_ref,
                    wup_ref, wdn_ref, gf_ref, o_ref, ys_ref, *, tm, final_norm):
    for t in range(SSM_CHUNK):
        for s in range(N_SLABS):
            lanes = slice(t * D_MODEL + s * LANES, t * D_MODEL + (s + 1) * LANES)
            rows = pl.ds(t, tm // SSM_CHUNK, stride=SSM_CHUNK)
            ys_ref[s, rows, :] = y_ref[:, lanes].astype(F32)
    y = jnp.concatenate([ys_ref[s] for s in range(N_SLABS)], axis=1)
    glu = _dot(jax.nn.gelu(y).astype(BF16), wglu_ref[...])
    ya = glu[:, :D_MODEL] * _sigmoid(glu[:, D_MODEL:])
    yb = _dot(c_ref[...], wpw_ref[...])
    gates = _sigmoid(gl_ref[...].astype(F32))
    mix = gates[:, :D_MODEL] * ya + gates[:, D_MODEL:] * yb
    x2 = x1_ref[...] + _dot(mix.astype(BF16), wout_ref[...])
    h = _rms_norm(x2, g2_ref[...]).astype(BF16)
    x3 = x2 + FFN_SCALE * _swiglu(h, wup_ref, wdn_ref)
    o_ref[...] = _rms_norm(x3, gf_ref[...]) if final_norm else x3


def _mix_ffn(x1, c, gl, y, wglu, wpw, wout, g2, wup, wdn, gf, tm, final_norm):
    n_tok = x1.shape[0]
    row = lambda w: pl.BlockSpec((tm, w), lambda i: (i, 0))
    consts = (wglu, wpw, wout, g2, wup, wdn, gf)
    y_spec = pl.BlockSpec((tm // SSM_CHUNK, SSM_CHUNK * D_MODEL), lambda i: (i, 0))
    return pl.pallas_call(
        functools.partial(_mix_ffn_kernel, tm=tm, final_norm=final_norm),
        grid=(n_tok // tm,),
        in_specs=[row(D_MODEL), row(D_MODEL), row(2 * D_MODEL), y_spec]
                 + [_const_spec(c.shape) for c in consts],
        out_specs=row(D_MODEL),
        out_shape=jax.ShapeDtypeStruct((n_tok, D_MODEL), F32),
        scratch_shapes=[pltpu.VMEM((N_SLABS, tm, LANES), F32)],
        compiler_params=pltpu.CompilerParams(
            dimension_semantics=("arbitrary",), vmem_limit_bytes=VMEM_LIMIT_BYTES),
        name="mix_ffn",
    )(x1, c, gl, y, *consts)


def _expand_kernel(b_ref, t_ref, c_ref, eb_ref, et_ref, mb_ref, mt_ref, mc_ref,
                   wb_ref, wt_ref, wc_ref):
    wb_ref[0] = (_dot(b_ref[0].astype(BF16), eb_ref[...]) * mb_ref[...]).astype(BF16)
    wt_ref[0] = (_dot(t_ref[0].astype(BF16), et_ref[...]) * mt_ref[...]).astype(BF16)
    wc_ref[0] = (_dot(c_ref[0].astype(BF16), et_ref[...]) * mc_ref[...]).astype(BF16)


def _expand_operands(wb_compact, wt_compact, wc_compact):
    m = SSM_CHUNK
    idx = lambda n: np.arange(n)
    src, dst = idx(2 * SSM_STATE)[:, None], idx(BLOCK_STATE)[None, :]
    spread_b = ((src // SSM_STATE == dst // LANES % 2)
                & (src % SSM_STATE == dst % SSM_STATE)).astype(BF16)
    row_group = idx(m * LANES)[:, None] % LANES // SSM_GROUP
    mask_b = ((row_group // 2 == dst // (2 * LANES))
              & (row_group % 2 == dst % LANES // SSM_STATE)).astype(F32)
    src, dst = idx(m * SSM_GROUP)[:, None], idx(m * LANES)[None, :]
    spread_t = ((src // SSM_GROUP == dst // LANES)
                & (src % SSM_GROUP == dst % SSM_GROUP)).astype(BF16)
    mask_t = (row_group == dst % LANES // SSM_GROUP).astype(F32)
    row = idx(BLOCK_STATE)[:, None]
    row_group = row // (2 * LANES) * 2 + row % LANES // SSM_STATE
    mask_c = (row_group == dst % LANES // SSM_GROUP).astype(F32)

    consts = (spread_b, spread_t, mask_b, mask_t, mask_c)
    per_block = lambda a: pl.BlockSpec((1,) + a.shape[1:], lambda i: (i, 0, 0))
    shapes = [(N_LANE_BLOCKS, m * LANES, BLOCK_STATE), (N_LANE_BLOCKS, m * LANES, m * LANES),
              (N_LANE_BLOCKS, BLOCK_STATE, m * LANES)]
    return pl.pallas_call(
        _expand_kernel,
        grid=(N_LANE_BLOCKS,),
        in_specs=[per_block(wb_compact), per_block(wt_compact), per_block(wc_compact)]
                 + [pl.BlockSpec(c.shape, lambda i: (0, 0)) for c in consts],
        out_specs=[pl.BlockSpec((1,) + s[1:], lambda i: (i, 0, 0)) for s in shapes],
        out_shape=[jax.ShapeDtypeStruct(s, BF16) for s in shapes],
        compiler_params=pltpu.CompilerParams(dimension_semantics=("arbitrary",)),
        name="expand_operands",
    )(wb_compact, wt_compact, wc_compact, *consts)


def _ssm_params(a_re, a_im, log_dt, b_re, b_im, c_re, c_im):
    hi = lax.Precision.HIGHEST
    m = SSM_CHUNK
    dt = jnp.exp(log_dt)[:, None]
    mag = jnp.exp(dt * a_re)
    abar_re = mag * jnp.cos(dt * a_im)
    abar_im = mag * jnp.sin(dt * a_im)
    den = a_re * a_re + a_im * a_im
    nr = abar_re - 1.0
    ni = abar_im
    f_re = (nr * a_re + ni * a_im) / den
    f_im = (ni * a_re - nr * a_im) / den
    bb_re = f_re[..., None] * b_re - f_im[..., None] * b_im
    bb_im = f_re[..., None] * b_im + f_im[..., None] * b_re

    pow_re, pow_im = [jnp.ones_like(abar_re)], [jnp.zeros_like(abar_re)]
    for _ in range(m):
        pr, pi = pow_re[-1], pow_im[-1]
        pow_re.append(pr * abar_re - pi * abar_im)
        pow_im.append(pr * abar_im + pi * abar_re)
    def times_pow(k, x_re, x_im):
        pr, pi = pow_re[k][..., None], pow_im[k][..., None]
        return x_re * pr - x_im * pi, x_re * pi + x_im * pr

    ct_re, ct_im = jnp.swapaxes(c_re, 1, 2), jnp.swapaxes(c_im, 1, 2)

    groups_per_block = LANES // SSM_GROUP
    nb = N_LANE_BLOCKS

    b_parts = [times_pow(m - 1 - t, bb_re, bb_im) for t in range(m)]
    x = jnp.stack([jnp.stack(p, axis=0) for p in b_parts], axis=0)
    x = x.reshape(m, 2, nb, groups_per_block, SSM_STATE, SSM_GROUP)
    wb_compact = x.transpose(2, 0, 3, 5, 1, 4).reshape(nb, m * LANES, 2 * SSM_STATE)

    c_parts = [times_pow(t + 1, ct_re, ct_im) for t in range(m)]
    x = jnp.stack([jnp.stack([p[0], -p[1]], axis=0) for p in c_parts], axis=0)
    x = x.reshape(m, 2, nb, PAIRS_PER_BLOCK, LANES, SSM_GROUP)
    wc_compact = x.transpose(2, 3, 1, 4, 0, 5).reshape(nb, BLOCK_STATE, m * SSM_GROUP)

    kern = []
    for k in range(m):
        n_re, n_im = times_pow(k, ct_re, ct_im)
        kern.append(jnp.einsum('gpo,gpi->gio', n_re, bb_re, precision=hi)
                    - jnp.einsum('gpo,gpi->gio', n_im, bb_im, precision=hi))
    zero = jnp.zeros_like(kern[0])
    x = jnp.stack([jnp.stack([kern[t - t0] if t0 <= t else zero for t in range(m)], axis=0)
                   for t0 in range(m)], axis=0)
    x = x.reshape(m, m, nb, groups_per_block, SSM_GROUP, SSM_GROUP)
    wt_compact = x.transpose(2, 0, 3, 4, 1, 5).reshape(nb, m * LANES, m * SSM_GROUP)

    wb, wt, wc = _expand_operands(wb_compact, wt_compact, wc_compact)

    ar = pow_re[m].reshape(N_SG, PAIRS_PER_SG, LANES)
    ai = pow_im[m].reshape(N_SG, PAIRS_PER_SG, LANES)
    return wb, wt, wc, ar, ai


def kernel(x, norm_ffn1_g, w_ffn1_up, w_ffn1_down, norm_mix_g, w_in, ssm_a_re, ssm_a_im, ssm_log_dt, ssm_b_re, ssm_b_im, ssm_c_re, ssm_c_im, ssm_d, w_ssm_glu, conv_dw_w, conv_dw_b, conv_ln_g, conv_ln_b, w_conv_pw, w_out, norm_ffn2_g, w_ffn2_up, w_ffn2_down, norm_final_g):
    bsz, seq, d = x.shape
    depth = w_in.shape[0]
    assert d == D_MODEL and w_ffn1_up.shape[-1] == 2 * D_FF
    tm = 512
    ts = 1024
    tm_mix = 512
    assert seq % tm == 0 and seq % ts == 0 and (bsz * seq) % tm_mix == 0
    row = lambda a: a.reshape(1, -1).astype(F32)

    x2d = x.reshape(bsz * seq, d)
    for l in range(depth):
        conv_w = jnp.pad(conv_dw_w[l].reshape(CONV_WIDTH, d), ((0, 1), (0, 0)))
        x1, u, gl, c = _ffn_conv(
            x2d, row(norm_ffn1_g[l]), w_ffn1_up[l].astype(BF16), w_ffn1_down[l].astype(BF16),
            row(norm_mix_g[l]), w_in[l].astype(BF16), conv_w, row(conv_dw_b[l]),
            row(conv_ln_g[l]), row(conv_ln_b[l]), tm, seq // tm)
        wb, wt, wc, ar, ai = _ssm_params(ssm_a_re[l], ssm_a_im[l], ssm_log_dt[l], ssm_b_re[l],
                                         ssm_b_im[l], ssm_c_re[l], ssm_c_im[l])
        d_chunk = jnp.tile(row(ssm_d[l]), (1, SSM_CHUNK))
        chunk_rows = seq // SSM_CHUNK
        y = _ssm_chunk(u.reshape(bsz, chunk_rows, SSM_CHUNK * d), wb, wt, wc, ar, ai, d_chunk, ts)
        x2d = _mix_ffn(x1, c, gl, y.reshape(bsz * chunk_rows, SSM_CHUNK * d),
                       w_ssm_glu[l].astype(BF16),
                       w_conv_pw[l].astype(BF16), w_out[l].astype(BF16), row(norm_ffn2_g[l]),
                       w_ffn2_up[l].astype(BF16), w_ffn2_down[l].astype(BF16),
                       row(norm_final_g), tm_mix, final_norm=(l == depth - 1))
    return x2d.reshape(bsz, seq, d)
```

```python
import functools

import jax
import jax.numpy as jnp
import numpy as np
from jax import lax
from jax.experimental import pallas as pl
from jax.experimental.pallas import tpu as pltpu

D_MODEL = 1024
D_FF = 2816
SSM_GROUP = 16
N_GROUPS = 64
SSM_STATE = 64
CONV_WIDTH = 31
RMS_EPS = 1e-6
LN_EPS = 1e-5
FFN_SCALE = 0.5

LANES = 128
SUBLANES = 8
MXU_DIM = 256
VMEM_LIMIT_BYTES = 56 * 1024 * 1024

PAIRS_PER_SG = SUBLANES
N_SG = N_GROUPS // (2 * PAIRS_PER_SG)
SSM_CHUNK = 4
N_LANE_BLOCKS = D_MODEL // LANES
PAIRS_PER_BLOCK = LANES // (2 * SSM_GROUP)
BLOCK_STATE = 2 * LANES * PAIRS_PER_BLOCK
STATE_PITCH = 12
SCAN_WIDTH = 2
N_SLABS = D_MODEL // LANES
CONV_HALO = 32
CONV_CHAINS = 2

FFN_CHUNKS = ((0, 1536), (1536, D_FF))

BF16 = jnp.bfloat16
F32 = jnp.float32


def _dot(a, b):
    return jnp.dot(a, b, preferred_element_type=F32)


def _rms_norm(x, g):
    return x * lax.rsqrt(jnp.mean(x * x, axis=-1, keepdims=True) + RMS_EPS) * g


def _sigmoid(x):
    return 1.0 / (1.0 + jnp.exp(-x))


def _zero_of(x):
    bits = lax.bitcast_convert_type(x, jnp.uint32)
    return lax.bitcast_convert_type((bits >> 16) >> 16, F32)


def _swiglu(h, wup_ref, wdn_ref):
    acc = None
    for c0, c1 in FFN_CHUNKS:
        gate = _dot(h, wup_ref[:, c0:c1])
        up = _dot(h, wup_ref[:, D_FF + c0:D_FF + c1])
        act = (gate * _sigmoid(gate) * up).astype(BF16)
        part = _dot(act, wdn_ref[c0:c1, :])
        acc = part if acc is None else acc + part
    return acc


def _const_spec(shape):
    nd = len(shape)
    return pl.BlockSpec(shape, lambda *_: (0,) * nd, pipeline_mode=pl.Buffered(1))


def _ffn_conv_kernel(x_ref, g1_ref, wup_ref, wdn_ref, gm_ref, win_ref,
                     cw_ref, cb_ref, lng_ref, lnb_ref,
                     x1_ref, u_ref, gl_ref, c_ref, z_ref, cv_ref, us_ref, *, tm, n_blk,
                     blocks_per_seq):
    g = pl.program_id(0)

    @pl.when(g == 0)
    def _():
        z_ref[...] = jnp.zeros_like(z_ref)

    def conv_branch():
        base = CONV_HALO - (CONV_WIDTH - 1)
        tiles = [(c, r) for c in range(N_SLABS) for r in range(tm // SUBLANES)]
        tiles_per_chain = len(tiles) // CONV_CHAINS
        prev = [None] * CONV_CHAINS
        for i in range(tiles_per_chain):
            for ch in range(CONV_CHAINS):
                c, r = tiles[ch * tiles_per_chain + i]
                lanes = slice(c * LANES, (c + 1) * LANES)
                acc = jnp.broadcast_to(cb_ref[:, lanes], (SUBLANES, LANES))
                if prev[ch] is not None:
                    acc = acc + _zero_of(prev[ch])
                for k in range(CONV_WIDTH):
                    rows = pl.ds(base + k + r * SUBLANES, SUBLANES, stride=1)
                    acc = acc + cw_ref[k:k + 1, lanes] * z_ref[c, rows, :]
                cv_ref[r * SUBLANES:(r + 1) * SUBLANES, lanes] = acc
                prev[ch] = acc

        cv = cv_ref[...]
        mu = jnp.mean(cv, axis=-1, keepdims=True)
        xc = cv - mu
        var = jnp.mean(xc * xc, axis=-1, keepdims=True)
        ln = xc * lax.rsqrt(var + LN_EPS) * lng_ref[...] + lnb_ref[...]
        c_ref[...] = (ln * _sigmoid(ln)).astype(c_ref.dtype)

    @pl.when(g == n_blk)
    def _():
        conv_branch()

    @pl.when(g < n_blk)
    def _():
        conv_branch()
        _ffn_and_projection(x_ref, g1_ref, wup_ref, wdn_ref, gm_ref, win_ref,
                            x1_ref, u_ref, gl_ref, z_ref, us_ref, g, tm, blocks_per_seq)


def _ffn_and_projection(x_ref, g1_ref, wup_ref, wdn_ref, gm_ref, win_ref,
                        x1_ref, u_ref, gl_ref, z_ref, us_ref, g, tm, blocks_per_seq):
    x = x_ref[...]
    h = _rms_norm(x, g1_ref[...]).astype(BF16)
    x1 = x + FFN_SCALE * _swiglu(h, wup_ref, wdn_ref)
    x1_ref[...] = x1
    h2 = _rms_norm(x1, gm_ref[...]).astype(BF16)
    u = _dot(h2, win_ref[:, 0:D_MODEL])
    for s in range(N_SLABS):
        us_ref[s] = u[:, s * LANES:(s + 1) * LANES]
    for t in range(SSM_CHUNK):
        for s in range(N_SLABS):
            lanes = slice(t * D_MODEL + s * LANES, t * D_MODEL + (s + 1) * LANES)
            rows = pl.ds(t, tm // SSM_CHUNK, stride=SSM_CHUNK)
            u_ref[:, lanes] = us_ref[s, rows, :].astype(u_ref.dtype)
    gl_ref[...] = _dot(h2, win_ref[:, 3 * D_MODEL:5 * D_MODEL]).astype(gl_ref.dtype)
    v = _dot(h2, win_ref[:, D_MODEL:3 * D_MODEL])
    z = v[:, :D_MODEL] * _sigmoid(v[:, D_MODEL:])

    seq_start = (g % blocks_per_seq) == 0
    for c in range(N_SLABS):
        tail = z_ref[c, tm:tm + CONV_HALO, :]
        z_ref[c, 0:CONV_HALO, :] = jnp.where(seq_start, 0.0, tail)
        z_ref[c, CONV_HALO:CONV_HALO + tm, :] = z[:, c * LANES:(c + 1) * LANES]


def _ffn_conv(x2d, g1, wup, wdn, gm, win, cw, cb, lng, lnb, tm, blocks_per_seq):
    n_tok = x2d.shape[0]
    n_blk = n_tok // tm
    cur = lambda w: pl.BlockSpec((tm, w), lambda i: (jnp.minimum(i, n_blk - 1), 0))
    prev = lambda w: pl.BlockSpec((tm, w), lambda i: (jnp.maximum(i - 1, 0), 0))
    consts = (g1, wup, wdn, gm, win, cw, cb, lng, lnb)
    return pl.pallas_call(
        functools.partial(_ffn_conv_kernel, tm=tm, n_blk=n_blk, blocks_per_seq=blocks_per_seq),
        grid=(n_blk + 1,),
        in_specs=[cur(D_MODEL)] + [_const_spec(c.shape) for c in consts],
        out_specs=[cur(D_MODEL),
                   pl.BlockSpec((tm // SSM_CHUNK, SSM_CHUNK * D_MODEL),
                                lambda i: (jnp.minimum(i, n_blk - 1), 0)),
                   cur(2 * D_MODEL), prev(D_MODEL)],
        out_shape=[jax.ShapeDtypeStruct((n_tok, D_MODEL), F32),
                   jax.ShapeDtypeStruct((n_tok // SSM_CHUNK, SSM_CHUNK * D_MODEL), BF16),
                   jax.ShapeDtypeStruct((n_tok, 2 * D_MODEL), BF16),
                   jax.ShapeDtypeStruct((n_tok, D_MODEL), BF16)],
        scratch_shapes=[pltpu.VMEM((N_SLABS, CONV_HALO + tm, LANES), F32),
                        pltpu.VMEM((tm, D_MODEL), F32),
                        pltpu.VMEM((N_SLABS, tm, LANES), F32)],
        compiler_params=pltpu.CompilerParams(
            dimension_semantics=("arbitrary",), vmem_limit_bytes=VMEM_LIMIT_BYTES),
        name="ffn_conv",
    )(x2d, *consts)


def _ssm_chunk_kernel(u_ref, wb_ref, wt_ref, wc_ref, ar_ref, ai_ref, d_ref,
                      y_ref, sre_ref, sim_ref, yt_ref, carry_ref, *, nc):
    @pl.when(pl.program_id(1) == 0)
    def _():
        carry_ref[...] = jnp.zeros_like(carry_ref)

    def chunk_lanes(blk):
        return [slice(t * D_MODEL + blk * LANES, t * D_MODEL + (blk + 1) * LANES)
                for t in range(SSM_CHUNK)]

    def pair_rows(blk, q):
        pair = blk * PAIRS_PER_BLOCK + q
        return pair // PAIRS_PER_SG, pl.ds(pair % PAIRS_PER_SG, nc, stride=STATE_PITCH)

    def chunk_inputs(blk):
        return jnp.concatenate([u_ref[0, :, l] for l in chunk_lanes(blk)], axis=1)

    for blk in range(N_LANE_BLOCKS):
        wbu = _dot(chunk_inputs(blk), wb_ref[blk])
        for q in range(PAIRS_PER_BLOCK):
            sg, rows = pair_rows(blk, q)
            c0 = 2 * LANES * q
            sre_ref[sg, rows, :] = wbu[:, c0:c0 + LANES]
            sim_ref[sg, rows, :] = wbu[:, c0 + LANES:c0 + 2 * LANES]

    carry = [carry_ref[i] for i in range(2 * N_SG)]
    dep = None
    for sg0 in range(0, N_SG, SCAN_WIDTH):
        if dep is not None:
            carry[2 * sg0] = carry[2 * sg0] + _zero_of(dep)
        coef = {sg: (ar_ref[sg], ai_ref[sg]) for sg in range(sg0, sg0 + SCAN_WIDTH)}
        for c in range(nc):
            rows = pl.ds(c * STATE_PITCH, PAIRS_PER_SG, stride=1)
            for sg, (ar, ai) in coef.items():
                sr, si = carry[2 * sg], carry[2 * sg + 1]
                nr = ar * sr - ai * si + sre_ref[sg, rows, :]
                ni = ar * si + ai * sr + sim_ref[sg, rows, :]
                sre_ref[sg, rows, :] = sr
                sim_ref[sg, rows, :] = si
                carry[2 * sg], carry[2 * sg + 1] = nr, ni
        dep = carry[2 * sg0]
    for i in range(2 * N_SG):
        carry_ref[i] = carry[i]

    for blk in range(N_LANE_BLOCKS):
        wtu = _dot(chunk_inputs(blk), wt_ref[blk])
        for t, l in enumerate(chunk_lanes(blk)):
            yt_ref[:, l] = wtu[:, t * LANES:(t + 1) * LANES] + d_ref[:, l] * u_ref[0, :, l].astype(F32)

    for blk in range(N_LANE_BLOCKS):
        parts = []
        for q in range(PAIRS_PER_BLOCK):
            sg, rows = pair_rows(blk, q)
            parts += [sre_ref[sg, rows, :].astype(BF16), sim_ref[sg, rows, :].astype(BF16)]
        wcs = _dot(jnp.concatenate(parts, axis=1), wc_ref[blk])
        for t, l in enumerate(chunk_lanes(blk)):
            y_ref[0, :, l] = (yt_ref[:, l] + wcs[:, t * LANES:(t + 1) * LANES]).astype(y_ref.dtype)


def _ssm_chunk(u_chunks, wb, wt, wc, ar, ai, d, tokens_per_step):
    bsz, n_chunks, width = u_chunks.shape
    nc = tokens_per_step // SSM_CHUNK
    consts = (wb, wt, wc, ar, ai, d)
    blk = pl.BlockSpec((1, nc, width), lambda b, i: (b, i, 0))
    state = pltpu.VMEM((N_SG, nc * STATE_PITCH, LANES), F32)
    return pl.pallas_call(
        functools.partial(_ssm_chunk_kernel, nc=nc),
        grid=(bsz, n_chunks // nc),
        in_specs=[blk] + [_const_spec(c.shape) for c in consts],
        out_specs=blk,
        out_shape=jax.ShapeDtypeStruct(u_chunks.shape, BF16),
        scratch_shapes=[state, state, pltpu.VMEM((nc, width), F32),
                        pltpu.VMEM((2 * N_SG, SUBLANES, LANES), F32)],
        compiler_params=pltpu.CompilerParams(
            dimension_semantics=("arbitrary", "arbitrary"),
            vmem_limit_bytes=VMEM_LIMIT_BYTES),
        name="ssm_chunk",
    )(u_chunks, *consts)


def _mix_ffn_kernel(x1_ref, c_ref, gl_ref, y_ref, wglu_ref, wpw_ref, wout_ref, g2_ref,
                    wup_ref, wdn_ref, gf_ref, o_ref, ys_ref, *, tm, final_norm):
    for t in range(SSM_CHUNK):
        for s in range(N_SLABS):
            lanes = slice(t * D_MODEL + s * LANES, t * D_MODEL + (s + 1) * LANES)
            rows = pl.ds(t, tm // SSM_CHUNK, stride=SSM_CHUNK)
            ys_ref[s, rows, :] = y_ref[:, lanes].astype(F32)
    y = jnp.concatenate([ys_ref[s] for s in range(N_SLABS)], axis=1)
    glu = _dot(jax.nn.gelu(y).astype(BF16), wglu_ref[...])
    ya = glu[:, :D_MODEL] * _sigmoid(glu[:, D_MODEL:])
    yb = _dot(c_ref[...], wpw_ref[...])
    gates = _sigmoid(gl_ref[...].astype(F32))
    mix = gates[:, :D_MODEL] * ya + gates[:, D_MODEL:] * yb
    x2 = x1_ref[...] + _dot(mix.astype(BF16), wout_ref[...])
    h = _rms_norm(x2, g2_ref[...]).astype(BF16)
    x3 = x2 + FFN_SCALE * _swiglu(h, wup_ref, wdn_ref)
    o_ref[...] = _rms_norm(x3, gf_ref[...]) if final_norm else x3


def _mix_ffn(x1, c, gl, y, wglu, wpw, wout, g2, wup, wdn, gf, tm, final_norm):
    n_tok = x1.shape[0]
    row = lambda w: pl.BlockSpec((tm, w), lambda i: (i, 0))
    consts = (wglu, wpw, wout, g2, wup, wdn, gf)
    y_spec = pl.BlockSpec((tm // SSM_CHUNK, SSM_CHUNK * D_MODEL), lambda i: (i, 0))
    return pl.pallas_call(
        functools.partial(_mix_ffn_kernel, tm=tm, final_norm=final_norm),
        grid=(n_tok // tm,),
        in_specs=[row(D_MODEL), row(D_MODEL), row(2 * D_MODEL), y_spec]
                 + [_const_spec(c.shape) for c in consts],
        out_specs=row(D_MODEL),
        out_shape=jax.ShapeDtypeStruct((n_tok, D_MODEL), F32),
        scratch_shapes=[pltpu.VMEM((N_SLABS, tm, LANES), F32)],
        compiler_params=pltpu.CompilerParams(
            dimension_semantics=("arbitrary",), vmem_limit_bytes=VMEM_LIMIT_BYTES),
        name="mix_ffn",
    )(x1, c, gl, y, *consts)


def _expand_kernel(b_ref, t_ref, c_ref, eb_ref, et_ref, mb_ref, mt_ref, mc_ref,
                   wb_ref, wt_ref, wc_ref):
    wb_ref[0] = (_dot(b_ref[0].astype(BF16), eb_ref[...]) * mb_ref[...]).astype(BF16)
    wt_ref[0] = (_dot(t_ref[0].astype(BF16), et_ref[...]) * mt_ref[...]).astype(BF16)
    wc_ref[0] = (_dot(c_ref[0].astype(BF16), et_ref[...]) * mc_ref[...]).astype(BF16)


def _expand_operands(wb_compact, wt_compact, wc_compact):
    m = SSM_CHUNK
    idx = lambda n: np.arange(n)
    src, dst = idx(2 * SSM_STATE)[:, None], idx(BLOCK_STATE)[None, :]
    spread_b = ((src // SSM_STATE == dst // LANES % 2)
                & (src % SSM_STATE == dst % SSM_STATE)).astype(BF16)
    row_group = idx(m * LANES)[:, None] % LANES // SSM_GROUP
    mask_b = ((row_group // 2 == dst // (2 * LANES))
              & (row_group % 2 == dst % LANES // SSM_STATE)).astype(F32)
    src, dst = idx(m * SSM_GROUP)[:, None], idx(m * LANES)[None, :]
    spread_t = ((src // SSM_GROUP == dst // LANES)
                & (src % SSM_GROUP == dst % SSM_GROUP)).astype(BF16)
    mask_t = (row_group == dst % LANES // SSM_GROUP).astype(F32)
    row = idx(BLOCK_STATE)[:, None]
    row_group = row // (2 * LANES) * 2 + row % LANES // SSM_STATE
    mask_c = (row_group == dst % LANES // SSM_GROUP).astype(F32)

    consts = (spread_b, spread_t, mask_b, mask_t, mask_c)
    per_block = lambda a: pl.BlockSpec((1,) + a.shape[1:], lambda i: (i, 0, 0))
    shapes = [(N_LANE_BLOCKS, m * LANES, BLOCK_STATE), (N_LANE_BLOCKS, m * LANES, m * LANES),
              (N_LANE_BLOCKS, BLOCK_STATE, m * LANES)]
    return pl.pallas_call(
        _expand_kernel,
        grid=(N_LANE_BLOCKS,),
        in_specs=[per_block(wb_compact), per_block(wt_compact), per_block(wc_compact)]
                 + [pl.BlockSpec(c.shape, lambda i: (0, 0)) for c in consts],
        out_specs=[pl.BlockSpec((1,) + s[1:], lambda i: (i, 0, 0)) for s in shapes],
        out_shape=[jax.ShapeDtypeStruct(s, BF16) for s in shapes],
        compiler_params=pltpu.CompilerParams(dimension_semantics=("arbitrary",)),
        name="expand_operands",
    )(wb_compact, wt_compact, wc_compact, *consts)


def _ssm_params(a_re, a_im, log_dt, b_re, b_im, c_re, c_im):
    hi = lax.Precision.HIGHEST
    m = SSM_CHUNK
    dt = jnp.exp(log_dt)[:, None]
    mag = jnp.exp(dt * a_re)
    abar_re = mag * jnp.cos(dt * a_im)
    abar_im = mag * jnp.sin(dt * a_im)
    den = a_re * a_re + a_im * a_im
    nr = abar_re - 1.0
    ni = abar_im
    f_re = (nr * a_re + ni * a_im) / den
    f_im = (ni * a_re - nr * a_im) / den
    bb_re = f_re[..., None] * b_re - f_im[..., None] * b_im
    bb_im = f_re[..., None] * b_im + f_im[..., None] * b_re

    pow_re, pow_im = [jnp.ones_like(abar_re)], [jnp.zeros_like(abar_re)]
    for _ in range(m):
        pr, pi = pow_re[-1], pow_im[-1]
        pow_re.append(pr * abar_re - pi * abar_im)
        pow_im.append(pr * abar_im + pi * abar_re)
    def times_pow(k, x_re, x_im):
        pr, pi = pow_re[k][..., None], pow_im[k][..., None]
        return x_re * pr - x_im * pi, x_re * pi + x_im * pr

    ct_re, ct_im = jnp.swapaxes(c_re, 1, 2), jnp.swapaxes(c_im, 1, 2)

    groups_per_block = LANES // SSM_GROUP
    nb = N_LANE_BLOCKS

    b_parts = [times_pow(m - 1 - t, bb_re, bb_im) for t in range(m)]
    x = jnp.stack([jnp.stack(p, axis=0) for p in b_parts], axis=0)
    x = x.reshape(m, 2, nb, groups_per_block, SSM_STATE, SSM_GROUP)
    wb_compact = x.transpose(2, 0, 3, 5, 1, 4).reshape(nb, m * LANES, 2 * SSM_STATE)

    c_parts = [times_pow(t + 1, ct_re, ct_im) for t in range(m)]
    x = jnp.stack([jnp.stack([p[0], -p[1]], axis=0) for p in c_parts], axis=0)
    x = x.reshape(m, 2, nb, PAIRS_PER_BLOCK, LANES, SSM_GROUP)
    wc_compact = x.transpose(2, 3, 1, 4, 0, 5).reshape(nb, BLOCK_STATE, m * SSM_GROUP)

    kern = []
    for k in range(m):
        n_re, n_im = times_pow(k, ct_re, ct_im)
        kern.append(jnp.einsum('gpo,gpi->gio', n_re, bb_re, precision=hi)
                    - jnp.einsum('gpo,gpi->gio', n_im, bb_im, precision=hi))
    zero = jnp.zeros_like(kern[0])
    x = jnp.stack([jnp.stack([kern[t - t0] if t0 <= t else zero for t in range(m)], axis=0)
                   for t0 in range(m)], axis=0)
    x = x.reshape(m, m, nb, groups_per_block, SSM_GROUP, SSM_GROUP)
    wt_compact = x.transpose(2, 0, 3, 4, 1, 5).reshape(nb, m * LANES, m * SSM_GROUP)

    wb, wt, wc = _expand_operands(wb_compact, wt_compact, wc_compact)

    ar = pow_re[m].reshape(N_SG, PAIRS_PER_SG, LANES)
    ai = pow_im[m].reshape(N_SG, PAIRS_PER_SG, LANES)
    return wb, wt, wc, ar, ai


def kernel(x, norm_ffn1_g, w_ffn1_up, w_ffn1_down, norm_mix_g, w_in, ssm_a_re, ssm_a_im, ssm_log_dt, ssm_b_re, ssm_b_im, ssm_c_re, ssm_c_im, ssm_d, w_ssm_glu, conv_dw_w, conv_dw_b, conv_ln_g, conv_ln_b, w_conv_pw, w_out, norm_ffn2_g, w_ffn2_up, w_ffn2_down, norm_final_g):
    bsz, seq, d = x.shape
    depth = w_in.shape[0]
    assert d == D_MODEL and w_ffn1_up.shape[-1] == 2 * D_FF
    tm = 512
    ts = 1024
    tm_mix = 512
    assert seq % tm == 0 and seq % ts == 0 and (bsz * seq) % tm_mix == 0
    row = lambda a: a.reshape(1, -1).astype(F32)

    x2d = x.reshape(bsz * seq, d)
    for l in range(depth):
        conv_w = jnp.pad(conv_dw_w[l].reshape(CONV_WIDTH, d), ((0, 1), (0, 0)))
        x1, u, gl, c = _ffn_conv(
            x2d, row(norm_ffn1_g[l]), w_ffn1_up[l].astype(BF16), w_ffn1_down[l].astype(BF16),
            row(norm_mix_g[l]), w_in[l].astype(BF16), conv_w, row(conv_dw_b[l]),
            row(conv_ln_g[l]), row(conv_ln_b[l]), tm, seq // tm)
        wb, wt, wc, ar, ai = _ssm_params(ssm_a_re[l], ssm_a_im[l], ssm_log_dt[l], ssm_b_re[l],
                                         ssm_b_im[l], ssm_c_re[l], ssm_c_im[l])
        d_chunk = jnp.tile(row(ssm_d[l]), (1, SSM_CHUNK))
        chunk_rows = seq // SSM_CHUNK
        y = _ssm_chunk(u.reshape(bsz, chunk_rows, SSM_CHUNK * d), wb, wt, wc, ar, ai, d_chunk, ts)
        x2d = _mix_ffn(x1, c, gl, y.reshape(bsz * chunk_rows, SSM_CHUNK * d),
                       w_ssm_glu[l].astype(BF16),
                       w_conv_pw[l].astype(BF16), w_out[l].astype(BF16), row(norm_ffn2_g[l]),
                       w_ffn2_up[l].astype(BF16), w_ffn2_down[l].astype(BF16),
                       row(norm_final_g), tm_mix, final_norm=(l == depth - 1))
    return x2d.reshape(bsz, seq, d)
```
